```python
import jax
import jax.numpy as jnp
from jax import lax
import numpy as np

D_MODEL = 2048
BATCH = 8
SEQ = 4096
DEPTH = 4

N_MIXERS = 2

A_CHUNK = 128
A_WIDTH = 2 * D_MODEL
A_GROUPS = 8
A_GROUP_DIM = A_WIDTH // A_GROUPS

B_HEADS = 4
B_KEY_DIM = D_MODEL // 2
B_VAL_DIM = D_MODEL
B_HEAD_K = B_KEY_DIM // B_HEADS
B_HEAD_V = B_VAL_DIM // B_HEADS
B_GATE_RANK = 16
B_GATE_NORMALIZER = 16.0
B_CHUNK = 64
B_IN_COLS = 2 * B_KEY_DIM + 2 * B_VAL_DIM + 2 * B_GATE_RANK

D_FF = 128 * ((8 * D_MODEL // 3 + 127) // 128)
N_EXPERTS = 8
TOP_K = 2
D_FF_EXPERT = D_FF

N_A_LAYERS = (DEPTH + 1) // 2
N_B_LAYERS = DEPTH // 2
N_DENSE_LAYERS = (DEPTH + 1) // 2
N_MOE_LAYERS = DEPTH // 2

DEEPNORM_ALPHA = (2.0 * DEPTH) ** 0.25
DEEPNORM_BETA = (8.0 * DEPTH) ** -0.25
LN_EPS = 1e-5
RMS_EPS = 1e-5

kernel_name = 'hybrid_gmlp_gla_moe_deepnorm_encoder'


def layer_norm(x, g, b):
    xf = x.astype(jnp.float32)
    mu = jnp.mean(xf, axis=-1, keepdims=True)
    var = jnp.mean(jnp.square(xf - mu), axis=-1, keepdims=True)
    y = (xf - mu) * lax.rsqrt(var + LN_EPS) * g.astype(jnp.float32) + b.astype(jnp.float32)
    return y.astype(x.dtype)


def rms_norm(x, g):
    xf = x.astype(jnp.float32)
    return xf * lax.rsqrt(jnp.mean(jnp.square(xf), axis=-1, keepdims=True) + RMS_EPS) * g.astype(jnp.float32)


def spatial_gating_mixer(x, w_in, v_ln_g, v_ln_b, w_s, b_s, w_out):
    bsz, seq, _ = x.shape
    z = jax.nn.gelu(x @ w_in, approximate=False)
    u, v = jnp.split(z, 2, axis=-1)
    v = layer_norm(v, v_ln_g, v_ln_b)
    v = v.reshape(bsz, seq // A_CHUNK, A_CHUNK, A_GROUPS, A_GROUP_DIM)
    v = jnp.einsum('gpq,bnqgc->bnpgc', w_s, v) + b_s.T[:, :, None]
    return (u * v.reshape(bsz, seq, A_WIDTH)) @ w_out


def gla_direction(q, k, v, log_a, diag_offset):
    bsz, nh, seq, dk = q.shape
    dv = v.shape[-1]
    n_chunks = seq // B_CHUNK

    def to_chunks(t):
        return t.reshape(bsz, nh, n_chunks, B_CHUNK, t.shape[-1])

    q, k, v, log_a = to_chunks(q), to_chunks(k), to_chunks(v), to_chunks(log_a)
    b = jnp.cumsum(log_a, axis=3)
    b_mid = b[:, :, :, B_CHUNK // 2 - 1:B_CHUNK // 2, :]
    q_rel = q * jnp.exp(b - b_mid)
    k_rel = k * jnp.exp(b_mid - b)
    scores = jnp.einsum('bhnik,bhnjk->bhnij', q_rel, k_rel)
    mask = jnp.tril(jnp.ones((B_CHUNK, B_CHUNK), dtype=bool), k=diag_offset)
    scores = jnp.where(mask, scores, 0.0)
    o_intra = jnp.einsum('bhnij,bhnjv->bhniv', scores, v)

    b_last = b[:, :, :, -1:, :]
    q_inter = jnp.moveaxis(q * jnp.exp(b), 2, 0)
    k_upd = jnp.moveaxis(k * jnp.exp(b_last - b), 2, 0)
    v_c = jnp.moveaxis(v, 2, 0)
    decay_chunk = jnp.moveaxis(jnp.exp(b_last[:, :, :, 0, :]), 2, 0)

    def step(state, inp):
        qi, ku, vi, dc = inp
        o = jnp.einsum('bhik,bhkv->bhiv', qi, state)
        state = dc[..., None] * state + jnp.einsum('bhjk,bhjv->bhkv', ku, vi)
        return state, o

    state0 = jnp.zeros((bsz, nh, dk, dv), jnp.float32)
    _, o_inter = lax.scan(step, state0, (q_inter, k_upd, v_c, decay_chunk))
    o = o_intra + jnp.moveaxis(o_inter, 0, 2)
    return o.reshape(bsz, nh, seq, dv)


def gla_mixer(x, w_in, w_gk_f, bias_gk_f, w_gk_b, bias_gk_b, norm_g, w_out):
    bsz, seq, _ = x.shape
    proj = x @ w_in
    splits = [B_KEY_DIM, 2 * B_KEY_DIM, 2 * B_KEY_DIM + B_VAL_DIM,
              2 * B_KEY_DIM + 2 * B_VAL_DIM, 2 * B_KEY_DIM + 2 * B_VAL_DIM + B_GATE_RANK]
    q, k, v, g, low_f, low_b = jnp.split(proj, splits, axis=-1)

    def heads(t, d):
        return t.reshape(bsz, seq, B_HEADS, d).transpose(0, 2, 1, 3).astype(jnp.float32)

    q = heads(q, B_HEAD_K) * (B_HEAD_K ** -0.5)
    k = heads(k, B_HEAD_K)
    v = heads(v, B_HEAD_V)
    la_f = heads(jax.nn.log_sigmoid((low_f @ w_gk_f + bias_gk_f).astype(jnp.float32)) / B_GATE_NORMALIZER, B_HEAD_K)
    la_b = heads(jax.nn.log_sigmoid((low_b @ w_gk_b + bias_gk_b).astype(jnp.float32)) / B_GATE_NORMALIZER, B_HEAD_K)

    o_fwd = gla_direction(q, k, v, la_f, 0)
    flip = lambda t: jnp.flip(t, axis=2)
    o_bwd = flip(gla_direction(flip(q), flip(k), flip(v), flip(la_b), -1))
    o = (o_fwd + o_bwd).transpose(0, 2, 1, 3)
    gate = jax.nn.silu(g.reshape(bsz, seq, B_HEADS, B_HEAD_V).astype(jnp.float32))
    o = rms_norm(o, norm_g) * gate
    return o.reshape(bsz, seq, B_VAL_DIM).astype(x.dtype) @ w_out


def swiglu(x, w1, w3, w2):
    return (jax.nn.silu(x @ w1) * (x @ w3)) @ w2


def moe_swiglu(x, router, w1, w3, w2):
    bsz, seq, d = x.shape
    xt = x.reshape(bsz * seq, d)
    logits = (xt @ router).astype(jnp.float32)
    top_vals, top_idx = lax.top_k(logits, TOP_K)
    gates = jax.nn.softmax(top_vals, axis=-1)
    combine = jnp.einsum('tk,tke->te', gates, jax.nn.one_hot(top_idx, N_EXPERTS, dtype=jnp.float32))
    combine = combine.astype(x.dtype)
    y = jnp.zeros_like(xt)
    for e in range(N_EXPERTS):
        y = y + combine[:, e:e + 1] * swiglu(xt, w1[e], w3[e], w2[e])
    return y.reshape(bsz, seq, d)


def setup_inputs(seed: int = 0) -> dict:
    key = jax.random.key(seed)
    ks = jax.random.split(key, 24)
    f32 = jnp.float32
    beta = DEEPNORM_BETA

    def nrm(k, shape, scale=1.0):
        return jax.random.normal(k, shape, f32) * scale

    x = nrm(ks[0], (BATCH, SEQ, D_MODEL))
    a_col = jnp.concatenate([jnp.full((A_WIDTH,), beta, f32), jnp.ones((A_WIDTH,), f32)])
    a_w_in = nrm(ks[1], (N_A_LAYERS, D_MODEL, 2 * A_WIDTH), D_MODEL ** -0.5) * a_col
    a_v_ln_g = 1.0 + nrm(ks[2], (N_A_LAYERS, A_WIDTH), 0.02)
    a_v_ln_b = nrm(ks[3], (N_A_LAYERS, A_WIDTH), 0.02)
    a_w_s = nrm(ks[4], (N_A_LAYERS, A_GROUPS, A_CHUNK, A_CHUNK), A_CHUNK ** -0.5)
    a_b_s = 1.0 + nrm(ks[5], (N_A_LAYERS, A_GROUPS, A_CHUNK), 0.02)
    a_w_out = nrm(ks[6], (N_A_LAYERS, A_WIDTH, D_MODEL), beta * A_WIDTH ** -0.5)

    b_col = jnp.concatenate([jnp.ones((2 * B_KEY_DIM,), f32), jnp.full((B_VAL_DIM,), beta, f32),
                             jnp.ones((B_VAL_DIM + 2 * B_GATE_RANK,), f32)])
    b_w_in = nrm(ks[7], (N_B_LAYERS, D_MODEL, B_IN_COLS), D_MODEL ** -0.5) * b_col
    b_w_gk_f = nrm(ks[8], (N_B_LAYERS, B_GATE_RANK, B_KEY_DIM), B_GATE_RANK ** -0.5)
    b_gk_bias_f = nrm(ks[9], (N_B_LAYERS, B_KEY_DIM), 0.1)
    b_w_gk_b = nrm(ks[10], (N_B_LAYERS, B_GATE_RANK, B_KEY_DIM), B_GATE_RANK ** -0.5)
    b_gk_bias_b = nrm(ks[11], (N_B_LAYERS, B_KEY_DIM), 0.1)
    b_norm_g = 1.0 + nrm(ks[12], (N_B_LAYERS, B_HEAD_V), 0.02)
    b_w_out = nrm(ks[13], (N_B_LAYERS, B_VAL_DIM, D_MODEL), beta * B_VAL_DIM ** -0.5)

    ffn_w1 = nrm(ks[14], (N_DENSE_LAYERS, D_MODEL, D_FF), D_MODEL ** -0.5)
    ffn_w3 = nrm(ks[15], (N_DENSE_LAYERS, D_MODEL, D_FF), beta * D_MODEL ** -0.5)
    ffn_w2 = nrm(ks[16], (N_DENSE_LAYERS, D_FF, D_MODEL), beta * D_FF ** -0.5)

    moe_router = nrm(ks[17], (N_MOE_LAYERS, D_MODEL, N_EXPERTS), D_MODEL ** -0.5)
    moe_w1 = nrm(ks[18], (N_MOE_LAYERS, N_EXPERTS, D_MODEL, D_FF_EXPERT), D_MODEL ** -0.5)
    moe_w3 = nrm(ks[19], (N_MOE_LAYERS, N_EXPERTS, D_MODEL, D_FF_EXPERT), beta * D_MODEL ** -0.5)
    moe_w2 = nrm(ks[20], (N_MOE_LAYERS, N_EXPERTS, D_FF_EXPERT, D_MODEL), beta * D_FF_EXPERT ** -0.5)

    ln_g = 1.0 + nrm(ks[21], (DEPTH, 2, D_MODEL), 0.02)
    ln_b = nrm(ks[22], (DEPTH, 2, D_MODEL), 0.02)
    return {'x': x, 'a_w_in': a_w_in, 'a_v_ln_g': a_v_ln_g, 'a_v_ln_b': a_v_ln_b,
            'a_w_s': a_w_s, 'a_b_s': a_b_s, 'a_w_out': a_w_out,
            'b_w_in': b_w_in, 'b_w_gk_f': b_w_gk_f, 'b_gk_bias_f': b_gk_bias_f,
            'b_w_gk_b': b_w_gk_b, 'b_gk_bias_b': b_gk_bias_b, 'b_norm_g': b_norm_g, 'b_w_out': b_w_out,
            'ffn_w1': ffn_w1, 'ffn_w3': ffn_w3, 'ffn_w2': ffn_w2,
            'moe_router': moe_router, 'moe_w1': moe_w1, 'moe_w3': moe_w3, 'moe_w2': moe_w2,
            'ln_g': ln_g, 'ln_b': ln_b}


def reference(x, a_w_in, a_v_ln_g, a_v_ln_b, a_w_s, a_b_s, a_w_out,
              b_w_in, b_w_gk_f, b_gk_bias_f, b_w_gk_b, b_gk_bias_b, b_norm_g, b_w_out,
              ffn_w1, ffn_w3, ffn_w2, moe_router, moe_w1, moe_w3, moe_w2, ln_g, ln_b):
    for i in range(DEPTH):
        j = i // N_MIXERS
        if i % N_MIXERS == 0:
            mix = spatial_gating_mixer(x, a_w_in[j], a_v_ln_g[j], a_v_ln_b[j], a_w_s[j], a_b_s[j], a_w_out[j])
        else:
            mix = gla_mixer(x, b_w_in[j], b_w_gk_f[j], b_gk_bias_f[j], b_w_gk_b[j], b_gk_bias_b[j],
                            b_norm_g[j], b_w_out[j])
        x = layer_norm(DEEPNORM_ALPHA * x + mix, ln_g[i, 0], ln_b[i, 0])
        c = i // 2
        if i % 2 == 0:
            ff = swiglu(x, ffn_w1[c], ffn_w3[c], ffn_w2[c])
        else:
            ff = moe_swiglu(x, moe_router[c], moe_w1[c], moe_w3[c], moe_w2[c])
        x = layer_norm(DEEPNORM_ALPHA * x + ff, ln_g[i, 1], ln_b[i, 1])
    return x
```

```python
import functools

import jax
import jax.numpy as jnp
from jax import lax
from jax.experimental import pallas as pl
from jax.experimental.pallas import tpu as pltpu

F32 = jnp.float32
BF16 = jnp.bfloat16

DEPTH = 4
A_CHUNK = 128
A_GROUPS = 8
B_HEADS = 4
B_GATE_RANK = 16
B_GATE_NORMALIZER = 16.0
B_CHUNK = 64
N_EXPERTS = 8
TOP_K = 2
DEEPNORM_ALPHA = (2.0 * DEPTH) ** 0.25
LN_EPS = 1e-5
RMS_EPS = 1e-5

LANE = 128
MXU_DIM = 256
VMEM_LIMIT_BYTES = 56 * 1024 * 1024


def _round_up(n, m):
    return (n + m - 1) // m * m


def _params(*semantics):
    return pltpu.CompilerParams(dimension_semantics=semantics,
                                vmem_limit_bytes=VMEM_LIMIT_BYTES)


def _layer_norm(y, g, b):
    mu = jnp.mean(y, axis=-1, keepdims=True)
    d = y - mu
    var = jnp.mean(d * d, axis=-1, keepdims=True)
    return d * lax.rsqrt(var + LN_EPS) * g + b


def _mm_act_kernel(x_ref, w_ref, o_ref, *, gelu):
    acc = jnp.dot(x_ref[...], w_ref[...], preferred_element_type=F32)
    if gelu:
        acc = 0.5 * acc * (1.0 + lax.erf(acc * (2.0 ** -0.5)))
    o_ref[...] = acc.astype(o_ref.dtype)


def _mm_act(x, w, *, gelu, out_dtype, tm, tn):
    m, k = x.shape
    n = w.shape[1]
    tm = min(tm, m)
    return pl.pallas_call(
        functools.partial(_mm_act_kernel, gelu=gelu),
        grid=(m // tm, n // tn),
        in_specs=[pl.BlockSpec((tm, k), lambda i, j: (i, 0)),
                  pl.BlockSpec((k, tn), lambda i, j: (0, j))],
        out_specs=pl.BlockSpec((tm, tn), lambda i, j: (i, j)),
        out_shape=jax.ShapeDtypeStruct((m, n), out_dtype),
        compiler_params=_params("parallel", "arbitrary"),
        name="mm_act",
    )(x, w)


def _mm_ln_kernel(h_ref, w_ref, r_ref, g_ref, b_ref, o_ref, o16_ref, acc_ref):
    k = pl.program_id(1)
    part = jnp.dot(h_ref[...], w_ref[...], preferred_element_type=F32)

    @pl.when(k == 0)
    def _():
        acc_ref[...] = part

    @pl.when(k > 0)
    def _():
        acc_ref[...] += part

    @pl.when(k == pl.num_programs(1) - 1)
    def _():
        y = _layer_norm(DEEPNORM_ALPHA * r_ref[...] + acc_ref[...], g_ref[...], b_ref[...])
        o_ref[...] = y
        o16_ref[...] = y.astype(BF16)


def _mm_ln(h, w, resid, g, b, *, tm, tk):
    m, kdim = h.shape
    n = w.shape[1]
    tm = min(tm, m)
    return pl.pallas_call(
        _mm_ln_kernel,
        grid=(m // tm, kdim // tk),
        in_specs=[pl.BlockSpec((tm, tk), lambda i, k: (i, k)),
                  pl.BlockSpec((tk, n), lambda i, k: (k, 0)),
                  pl.BlockSpec((tm, n), lambda i, k: (i, 0)),
                  pl.BlockSpec((1, n), lambda i, k: (0, 0)),
                  pl.BlockSpec((1, n), lambda i, k: (0, 0))],
        out_specs=[pl.BlockSpec((tm, n), lambda i, k: (i, 0)),
                   pl.BlockSpec((tm, n), lambda i, k: (i, 0))],
        out_shape=[jax.ShapeDtypeStruct((m, n), F32),
                   jax.ShapeDtypeStruct((m, n), BF16)],
        scratch_shapes=[pltpu.VMEM((tm, n), F32)],
        compiler_params=_params("parallel", "arbitrary"),
        name="mm_ln",
    )(h, w, resid, g.reshape(1, n), b.reshape(1, n))


def _spatial_gate_kernel(z_ref, g_ref, b_ref, ws_ref, bs_ref, o_ref, *, width, chunks):
    gdim = width // A_GROUPS
    v = z_ref[:, width:].astype(F32)
    vn = _layer_norm(v, g_ref[...], b_ref[...]).astype(BF16)
    for grp in range(A_GROUPS):
        w = ws_ref[grp]
        bias = bs_ref[:, grp:grp + 1]
        cols = slice(grp * gdim, (grp + 1) * gdim)
        for c in range(chunks):
            rows = slice(c * A_CHUNK, (c + 1) * A_CHUNK)
            s = jnp.dot(w, vn[rows, cols], preferred_element_type=F32) + bias
            u = z_ref[rows, cols].astype(F32)
            o_ref[rows, cols] = (u * s).astype(BF16)


def _spatial_gate(z, ln_g, ln_b, w_s, b_s_t, *, chunks):
    m = z.shape[0]
    width = z.shape[1] // 2
    tm = chunks * A_CHUNK
    return pl.pallas_call(
        functools.partial(_spatial_gate_kernel, width=width, chunks=chunks),
        grid=(m // tm,),
        in_specs=[pl.BlockSpec((tm, 2 * width), lambda i: (i, 0)),
                  pl.BlockSpec((1, width), lambda i: (0, 0)),
                  pl.BlockSpec((1, width), lambda i: (0, 0)),
                  pl.BlockSpec((A_GROUPS, A_CHUNK, A_CHUNK), lambda i: (0, 0, 0)),
                  pl.BlockSpec((A_CHUNK, A_GROUPS), lambda i: (0, 0))],
        out_specs=pl.BlockSpec((tm, width), lambda i: (i, 0)),
        out_shape=jax.ShapeDtypeStruct((m, width), BF16),
        compiler_params=_params("parallel"),
        name="spatial_gate",
    )(z, ln_g.reshape(1, width), ln_b.reshape(1, width), w_s, b_s_t)


def _gla_kernel(*refs, reverse, n_chunks):
    if reverse:
        (q_ref, k_ref, v_ref, low_ref, wgk_ref, bias_ref,
         ofwd_ref, gate_ref, ng_ref, o_ref, state_ref) = refs
    else:
        q_ref, k_ref, v_ref, low_ref, wgk_ref, bias_ref, o_ref, state_ref = refs
    dk = q_ref.shape[-1]
    c_len = B_CHUNK

    @pl.when(pl.program_id(2) == 0)
    def _():
        state_ref[...] = jnp.zeros_like(state_ref)

    row = lax.broadcasted_iota(jnp.int32, (c_len, c_len), 0)
    col = lax.broadcasted_iota(jnp.int32, (c_len, c_len), 1)
    if reverse:
        cum = (col >= row).astype(F32)
        keep = col > row
        mid, last = c_len // 2, 0
    else:
        cum = (col <= row).astype(F32)
        keep = col <= row
        mid, last = c_len // 2 - 1, c_len - 1

    def chunk(ci, carry):
        c = (n_chunks - 1 - ci) if reverse else ci
        rows = pl.ds(pl.multiple_of(c * c_len, c_len), c_len)
        q = q_ref[0, rows, :] * (dk ** -0.5)
        k = k_ref[0, rows, :]
        v = v_ref[0, rows, :].astype(BF16)
        low = low_ref[0, rows, :].astype(BF16)
        pre = jnp.dot(low, wgk_ref[...], preferred_element_type=F32) + bias_ref[...]
        log_a = jax.nn.log_sigmoid(pre) / B_GATE_NORMALIZER
        b = jnp.dot(cum, log_a, preferred_element_type=F32, precision=lax.Precision.HIGHEST)
        b_mid = b[mid:mid + 1, :]
        b_last = b[last:last + 1, :]
        q_rel = (q * jnp.exp(b - b_mid)).astype(BF16)
        k_rel = (k * jnp.exp(b_mid - b)).astype(BF16)
        scores = lax.dot_general(q_rel, k_rel, (((1,), (1,)), ((), ())),
                                 preferred_element_type=F32)
        scores = jnp.where(keep, scores, 0.0).astype(BF16)
        o = jnp.dot(scores, v, preferred_element_type=F32)
        q_inter = (q * jnp.exp(b)).astype(BF16)
        k_upd = (k * jnp.exp(b_last - b)).astype(BF16)
        state = state_ref[...]
        o = o + lax.dot_general(q_inter, state.astype(BF16), (((1,), (1,)), ((), ())),
                                preferred_element_type=F32)
        state_ref[...] = state * jnp.exp(b_last) + lax.dot_general(
            v, k_upd, (((0,), (0,)), ((), ())), preferred_element_type=F32)
        if reverse:
            o = o + ofwd_ref[0, rows, :]
            o = o * lax.rsqrt(jnp.mean(o * o, axis=-1, keepdims=True) + RMS_EPS) * ng_ref[...]
            o = o * jax.nn.silu(gate_ref[0, rows, :])
        o_ref[0, rows, :] = o.astype(o_ref.dtype)
        return carry

    lax.fori_loop(0, n_chunks, chunk, 0)


def _gla_direction(proj, wgk, bias, *, reverse, blk, o_fwd=None, norm_g=None):
    bsz, seq, _ = proj.shape
    dk = wgk.shape[1] // B_HEADS
    dv = 2 * dk
    key_dim = dk * B_HEADS
    val_dim = dv * B_HEADS
    blk = min(blk, seq)
    nb = seq // blk
    low_block = (2 * key_dim + 2 * val_dim) // LANE

    def sblk(n):
        return (nb - 1 - n) if reverse else n

    in_specs = [
        pl.BlockSpec((1, blk, dk), lambda b, h, n: (b, sblk(n), h)),
        pl.BlockSpec((1, blk, dk), lambda b, h, n: (b, sblk(n), key_dim // dk + h)),
        pl.BlockSpec((1, blk, dv), lambda b, h, n: (b, sblk(n), 2 * key_dim // dv + h)),
        pl.BlockSpec((1, blk, LANE), lambda b, h, n: (b, sblk(n), low_block)),
        pl.BlockSpec((LANE, dk), lambda b, h, n: (0, h)),
        pl.BlockSpec((1, dk), lambda b, h, n: (0, h)),
    ]
    args = [proj, proj, proj, proj, wgk, bias.reshape(1, key_dim)]
    if reverse:
        in_specs += [
            pl.BlockSpec((1, blk, dv), lambda b, h, n: (b, sblk(n), h)),
            pl.BlockSpec((1, blk, dv), lambda b, h, n: (b, sblk(n), (2 * key_dim + val_dim) // dv + h)),
            pl.BlockSpec((1, dv), lambda b, h, n: (0, 0)),
        ]
        args += [o_fwd, proj, norm_g.reshape(1, dv)]
    return pl.pallas_call(
        functools.partial(_gla_kernel, reverse=reverse, n_chunks=blk // B_CHUNK),
        grid=(bsz, B_HEADS, nb),
        in_specs=in_specs,
        out_specs=pl.BlockSpec((1, blk, dv), lambda b, h, n: (b, sblk(n), h)),
        out_shape=jax.ShapeDtypeStruct((bsz, seq, val_dim), BF16 if reverse else F32),
        scratch_shapes=[pltpu.VMEM((dv, dk), F32)],
        compiler_params=_params("parallel", "parallel", "arbitrary"),
        name="gla_bwd" if reverse else "gla_fwd",
    )(*args)


def _ffn_kernel(te_ref, tv_ref, *refs, fuse_ln):
    if fuse_ln:
        x_ref, w1_ref, w3_ref, w2_ref, r_ref, g_ref, b_ref, o_ref, o16_ref, acc_ref = refs
    else:
        x_ref, w1_ref, w3_ref, w2_ref, o_ref, acc_ref = refs
    i = pl.program_id(0)
    j = pl.program_id(1)
    last = pl.num_programs(1) - 1
    valid = tv_ref[i] != 0

    @pl.when(valid)
    def _():
        x = x_ref[...]
        a = jnp.dot(x, w1_ref[0], preferred_element_type=F32)
        c = jnp.dot(x, w3_ref[0], preferred_element_type=F32)
        h = (jax.nn.silu(a) * c).astype(BF16)
        part = jnp.dot(h, w2_ref[0], preferred_element_type=F32)

        @pl.when(j == 0)
        def _():
            acc_ref[...] = part

        @pl.when(j > 0)
        def _():
            acc_ref[...] += part

    if fuse_ln:
        @pl.when(j == last)
        def _():
            y = _layer_norm(DEEPNORM_ALPHA * r_ref[...] + acc_ref[...], g_ref[...], b_ref[...])
            o_ref[...] = y
            o16_ref[...] = y.astype(BF16)
    else:
        @pl.when(jnp.logical_and(j == last, valid))
        def _():
            o_ref[...] = acc_ref[...]

        @pl.when(jnp.logical_and(j == last, jnp.logical_not(valid)))
        def _():
            o_ref[...] = jnp.zeros_like(o_ref)


def _ffn(xs, w1, w3, w2, tile_expert, tile_valid, *, tm, tf, ln=None):
    rows, d = xs.shape
    f = w1.shape[2]
    nj = f // tf
    fuse_ln = ln is not None

    def wcol(i, j, te, tv):
        return jnp.where(tv[i] != 0, j, nj - 1)

    in_specs = [
        pl.BlockSpec((tm, d), lambda i, j, te, tv: (i, 0)),
        pl.BlockSpec((1, d, tf), lambda i, j, te, tv: (te[i], 0, wcol(i, j, te, tv))),
        pl.BlockSpec((1, d, tf), lambda i, j, te, tv: (te[i], 0, wcol(i, j, te, tv))),
        pl.BlockSpec((1, tf, d), lambda i, j, te, tv: (te[i], wcol(i, j, te, tv), 0)),
    ]
    args = [xs, w1, w3, w2]
    out_specs = [pl.BlockSpec((tm, d), lambda i, j, te, tv: (i, 0))]
    out_shape = [jax.ShapeDtypeStruct((rows, d), F32)]
    if fuse_ln:
        resid, g, b = ln
        in_specs += [pl.BlockSpec((tm, d), lambda i, j, te, tv: (i, 0)),
                     pl.BlockSpec((1, d), lambda i, j, te, tv: (0, 0)),
                     pl.BlockSpec((1, d), lambda i, j, te, tv: (0, 0))]
        args += [resid, g.reshape(1, d), b.reshape(1, d)]
        out_specs.append(pl.BlockSpec((tm, d), lambda i, j, te, tv: (i, 0)))
        out_shape.append(jax.ShapeDtypeStruct((rows, d), BF16))
    out = pl.pallas_call(
        functools.partial(_ffn_kernel, fuse_ln=fuse_ln),
        grid_spec=pltpu.PrefetchScalarGridSpec(
            num_scalar_prefetch=2,
            grid=(rows // tm, nj),
            in_specs=in_specs,
            out_specs=out_specs,
            scratch_shapes=[pltpu.VMEM((tm, d), F32)]),
        out_shape=out_shape,
        compiler_params=_params("parallel", "arbitrary"),
        name="ffn_ln" if fuse_ln else "ffn_experts",
    )(tile_expert, tile_valid, *args)
    return out if fuse_ln else out[0]


def _router_kernel(x_ref, w_ref, gate_ref, idx_ref):
    logits = jnp.dot(x_ref[...], w_ref[...], preferred_element_type=F32)
    col = lax.broadcasted_iota(jnp.int32, logits.shape, 1)
    neg = jnp.float32(-jnp.inf)
    logits = jnp.where(col < N_EXPERTS, logits, neg)
    m1 = jnp.max(logits, axis=-1, keepdims=True)
    i1 = jnp.min(jnp.where(logits == m1, col, LANE), axis=-1, keepdims=True)
    rest = jnp.where(col == i1, neg, logits)
    m2 = jnp.max(rest, axis=-1, keepdims=True)
    i2 = jnp.min(jnp.where(rest == m2, col, LANE), axis=-1, keepdims=True)
    e1 = jnp.exp(m1 - m1)
    e2 = jnp.exp(m2 - m1)
    total = e1 + e2
    gate_ref[...] = jnp.where(col == 0, e1 / total, jnp.where(col == 1, e2 / total, 0.0))
    idx_ref[...] = jnp.where(col == 0, i1, jnp.where(col == 1, i2, 0))


def _router(x16, w_pad, *, tm):
    m, d = x16.shape
    tm = min(tm, m)
    return pl.pallas_call(
        _router_kernel,
        grid=(m // tm,),
        in_specs=[pl.BlockSpec((tm, d), lambda i: (i, 0)),
                  pl.BlockSpec((d, LANE), lambda i: (0, 0))],
        out_specs=[pl.BlockSpec((tm, LANE), lambda i: (i, 0)),
                   pl.BlockSpec((tm, LANE), lambda i: (i, 0))],
        out_shape=[jax.ShapeDtypeStruct((m, LANE), F32),
                   jax.ShapeDtypeStruct((m, LANE), jnp.int32)],
        compiler_params=_params("parallel"),
        name="router",
    )(x16, w_pad)


def _combine_ln_kernel(r_ref, y0_ref, y1_ref, gate_ref, g_ref, b_ref, o_ref, o16_ref):
    gates = gate_ref[...]
    ff = gates[:, 0:1] * y0_ref[...] + gates[:, 1:2] * y1_ref[...]
    y = _layer_norm(DEEPNORM_ALPHA * r_ref[...] + ff, g_ref[...], b_ref[...])
    o_ref[...] = y
    o16_ref[...] = y.astype(BF16)


def _combine_ln(resid, y0, y1, gates, g, b, *, tm):
    m, d = resid.shape
    tm = min(tm, m)
    row = pl.BlockSpec((tm, d), lambda i: (i, 0))
    vec = pl.BlockSpec((1, d), lambda i: (0, 0))
    return pl.pallas_call(
        _combine_ln_kernel,
        grid=(m // tm,),
        in_specs=[row, row, row, pl.BlockSpec((tm, LANE), lambda i: (i, 0)), vec, vec],
        out_specs=[row, row],
        out_shape=[jax.ShapeDtypeStruct((m, d), F32), jax.ShapeDtypeStruct((m, d), BF16)],
        compiler_params=_params("parallel"),
        name="combine_ln",
    )(resid, y0, y1, gates, g.reshape(1, d), b.reshape(1, d))


def _pad_ff(w, axis, f_pad):
    pad = [(0, 0)] * w.ndim
    pad[axis] = (0, f_pad - w.shape[axis])
    return jnp.pad(w.astype(BF16), pad)


def _gmlp_layer(x, x16, w_in, v_ln_g, v_ln_b, w_s, b_s, w_out, ln_g, ln_b):
    z = _mm_act(x16, w_in.astype(BF16), gelu=True, out_dtype=BF16, tm=1024, tn=1024)
    h = _spatial_gate(z, v_ln_g, v_ln_b, w_s.astype(BF16), b_s.T, chunks=2)
    return _mm_ln(h, w_out.astype(BF16), x, ln_g, ln_b, tm=512, tk=1024)


def _gla_layer(x, x16, bsz, w_in, w_gk_f, bias_f, w_gk_b, bias_b, norm_g, w_out, ln_g, ln_b):
    t, d = x.shape
    seq = t // bsz
    key_dim = w_gk_f.shape[1]
    main = w_in.shape[1] - 2 * B_GATE_RANK
    cols = _round_up(main + LANE, 5 * MXU_DIM)
    w_in_p = jnp.pad(w_in.astype(BF16), ((0, 0), (0, cols - w_in.shape[1])))
    proj = _mm_act(x16, w_in_p, gelu=False, out_dtype=F32, tm=1024, tn=cols // 5)
    proj = proj.reshape(bsz, seq, cols)
    wf = jnp.zeros((LANE, key_dim), BF16).at[:B_GATE_RANK].set(w_gk_f.astype(BF16))
    wb = jnp.zeros((LANE, key_dim), BF16).at[B_GATE_RANK:2 * B_GATE_RANK].set(w_gk_b.astype(BF16))
    o_fwd = _gla_direction(proj, wf, bias_f, reverse=False, blk=512)
    h = _gla_direction(proj, wb, bias_b, reverse=True, blk=512, o_fwd=o_fwd, norm_g=norm_g)
    return _mm_ln(h.reshape(t, -1), w_out.astype(BF16), x, ln_g, ln_b, tm=512, tk=1024)


def _dense_ffn_layer(x, x16, w1, w3, w2, ln_g, ln_b, *, tm, tf):
    t = x.shape[0]
    f_pad = _round_up(w1.shape[1], tf)
    tm = min(tm, t)
    n_tiles = t // tm
    return _ffn(x16, _pad_ff(w1, 1, f_pad)[None], _pad_ff(w3, 1, f_pad)[None], _pad_ff(w2, 0, f_pad)[None],
                jnp.zeros((n_tiles,), jnp.int32), jnp.ones((n_tiles,), jnp.int32),
                tm=tm, tf=tf, ln=(x, ln_g, ln_b))


def _moe_layer(x, x16, router, w1, w3, w2, ln_g, ln_b, *, tm, tf):
    t, d = x.shape
    tm = min(tm, t)
    router_p = jnp.pad(router.astype(BF16), ((0, 0), (0, LANE - N_EXPERTS)))
    gates, idx = _router(x16, router_p, tm=1024)

    pair_expert = idx[:, :TOP_K].reshape(-1)
    onehot = (pair_expert[:, None] == jnp.arange(N_EXPERTS, dtype=jnp.int32)[None, :]).astype(jnp.int32)
    rank = jnp.take_along_axis(jnp.cumsum(onehot, axis=0) - onehot, pair_expert[:, None], axis=1)[:, 0]
    counts = jnp.sum(onehot, axis=0)
    padded = (counts + tm - 1) // tm * tm
    ends = jnp.cumsum(padded)
    starts = ends - padded
    pair_row = starts[pair_expert] + rank
    n_tiles = (t * TOP_K) // tm + N_EXPERTS
    rows = n_tiles * tm
    row_token = jnp.zeros((rows,), jnp.int32).at[pair_row].set(
        jnp.arange(t * TOP_K, dtype=jnp.int32) // TOP_K)
    tile_start = jnp.arange(n_tiles, dtype=jnp.int32) * tm
    tile_valid = (tile_start < ends[-1]).astype(jnp.int32)
    tile_expert = jnp.minimum(jnp.sum((tile_start[:, None] >= ends[None, :]).astype(jnp.int32), axis=1),
                              N_EXPERTS - 1)
    last_expert = jnp.max(jnp.where(counts > 0, jnp.arange(N_EXPERTS, dtype=jnp.int32), 0))
    tile_expert = jnp.where(tile_valid != 0, tile_expert, last_expert)

    f_pad = _round_up(w1.shape[2], tf)
    xs = jnp.take(x16, row_token, axis=0)
    ys = _ffn(xs, _pad_ff(w1, 2, f_pad), _pad_ff(w3, 2, f_pad), _pad_ff(w2, 1, f_pad),
              tile_expert, tile_valid, tm=tm, tf=tf)
    pos = pair_row.reshape(t, TOP_K)
    y0 = jnp.take(ys, pos[:, 0], axis=0)
    y1 = jnp.take(ys, pos[:, 1], axis=0)
    return _combine_ln(x, y0, y1, gates, ln_g, ln_b, tm=512)


def kernel(x, a_w_in, a_v_ln_g, a_v_ln_b, a_w_s, a_b_s, a_w_out, b_w_in, b_w_gk_f, b_gk_bias_f, b_w_gk_b, b_gk_bias_b, b_norm_g, b_w_out, ffn_w1, ffn_w3, ffn_w2, moe_router, moe_w1, moe_w3, moe_w2, ln_g, ln_b):
    bsz, seq, d = x.shape
    x = x.reshape(bsz * seq, d)
    x16 = x.astype(BF16)
    for i in range(DEPTH):
        j = i // 2
        if i % 2 == 0:
            x, x16 = _gmlp_layer(x, x16, a_w_in[j], a_v_ln_g[j], a_v_ln_b[j], a_w_s[j], a_b_s[j],
                                 a_w_out[j], ln_g[i, 0], ln_b[i, 0])
            x, x16 = _dense_ffn_layer(x, x16, ffn_w1[j], ffn_w3[j], ffn_w2[j], ln_g[i, 1], ln_b[i, 1],
                                      tm=512, tf=512)
        else:
            x, x16 = _gla_layer(x, x16, bsz, b_w_in[j], b_w_gk_f[j], b_gk_bias_f[j], b_w_gk_b[j],
                                b_gk_bias_b[j], b_norm_g[j], b_w_out[j], ln_g[i, 0], ln_b[i, 0])
            x, x16 = _moe_layer(x, x16, moe_router[j], moe_w1[j], moe_w3[j], moe_w2[j],
                                ln_g[i, 1], ln_b[i, 1], tm=512, tf=512)
    return x.reshape(bsz, seq, d)
```

```python
import functools

import jax
import jax.numpy as jnp
from jax import lax
from jax.experimental import pallas as pl
from jax.experimental.pallas import tpu as pltpu

F32 = jnp.float32
BF16 = jnp.bfloat16

DEPTH = 4
A_CHUNK = 128
A_GROUPS = 8
B_HEADS = 4
B_GATE_RANK = 16
B_GATE_NORMALIZER = 16.0
B_CHUNK = 64
N_EXPERTS = 8
TOP_K = 2
DEEPNORM_ALPHA = (2.0 * DEPTH) ** 0.25
LN_EPS = 1e-5
RMS_EPS = 1e-5

LANE = 128
MXU_DIM = 256
VMEM_LIMIT_BYTES = 56 * 1024 * 1024


def _round_up(n, m):
    return (n + m - 1) // m * m


def _params(*semantics):
    return pltpu.CompilerParams(dimension_semantics=semantics,
                                vmem_limit_bytes=VMEM_LIMIT_BYTES)


def _layer_norm(y, g, b):
    mu = jnp.mean(y, axis=-1, keepdims=True)
    d = y - mu
    var = jnp.mean(d * d, axis=-1, keepdims=True)
    return d * lax.rsqrt(var + LN_EPS) * g + b


def _mm_act_kernel(x_ref, w_ref, o_ref, *, gelu):
    acc = jnp.dot(x_ref[...], w_ref[...], preferred_element_type=F32)
    if gelu:
        acc = 0.5 * acc * (1.0 + lax.erf(acc * (2.0 ** -0.5)))
    o_ref[...] = acc.astype(o_ref.dtype)


def _mm_act(x, w, layer, *, gelu, out_dtype, tm, tn):
    m, k = x.shape
    n = w.shape[2]
    tm = min(tm, m)
    return pl.pallas_call(
        functools.partial(_mm_act_kernel, gelu=gelu),
        grid=(m // tm, n // tn),
        in_specs=[pl.BlockSpec((tm, k), lambda i, j: (i, 0)),
                  pl.BlockSpec((None, k, tn), lambda i, j: (layer, 0, j))],
        out_specs=pl.BlockSpec((tm, tn), lambda i, j: (i, j)),
        out_shape=jax.ShapeDtypeStruct((m, n), out_dtype),
        compiler_params=_params("parallel", "arbitrary"),
        name="mm_act",
    )(x, w)


def _mm_ln_kernel(h_ref, w_ref, r_ref, g_ref, b_ref, o_ref, o16_ref, acc_ref):
    k = pl.program_id(1)

    @pl.when(k == 0)
    def _():
        acc_ref[...] = jnp.zeros_like(acc_ref)

    acc_ref[...] += jnp.dot(h_ref[...], w_ref[...], preferred_element_type=F32)

    @pl.when(k == pl.num_programs(1) - 1)
    def _():
        y = _layer_norm(DEEPNORM_ALPHA * r_ref[...] + acc_ref[...], g_ref[...], b_ref[...])
        o_ref[...] = y
        o16_ref[...] = y.astype(BF16)


def _mm_ln(h, w, layer, resid, g, b, *, tm, tk):
    m, kdim = h.shape
    n = w.shape[2]
    tm = min(tm, m)
    return pl.pallas_call(
        _mm_ln_kernel,
        grid=(m // tm, kdim // tk),
        in_specs=[pl.BlockSpec((tm, tk), lambda i, k: (i, k)),
                  pl.BlockSpec((None, tk, n), lambda i, k: (layer, k, 0)),
                  pl.BlockSpec((tm, n), lambda i, k: (i, 0)),
                  pl.BlockSpec((1, n), lambda i, k: (0, 0)),
                  pl.BlockSpec((1, n), lambda i, k: (0, 0))],
        out_specs=[pl.BlockSpec((tm, n), lambda i, k: (i, 0)),
                   pl.BlockSpec((tm, n), lambda i, k: (i, 0))],
        out_shape=[jax.ShapeDtypeStruct((m, n), F32),
                   jax.ShapeDtypeStruct((m, n), BF16)],
        scratch_shapes=[pltpu.VMEM((tm, n), F32)],
        compiler_params=_params("parallel", "arbitrary"),
        name="mm_ln",
    )(h, w, resid, g.reshape(1, n), b.reshape(1, n))


def _spatial_gate_kernel(z_ref, g_ref, b_ref, ws_ref, bs_ref, o_ref, *, width, chunks):
    gdim = width // A_GROUPS
    v = z_ref[:, width:].astype(F32)
    vn = _layer_norm(v, g_ref[...], b_ref[...]).astype(BF16)
    for grp in range(A_GROUPS):
        w = ws_ref[grp]
        bias = bs_ref[:, grp:grp + 1]
        cols = slice(grp * gdim, (grp + 1) * gdim)
        for c in range(chunks):
            rows = slice(c * A_CHUNK, (c + 1) * A_CHUNK)
            s = jnp.dot(w, vn[rows, cols], preferred_element_type=F32) + bias
            u = z_ref[rows, cols].astype(F32)
            o_ref[rows, cols] = (u * s).astype(BF16)


def _spatial_gate(z, ln_g, ln_b, w_s, b_s_t, *, chunks):
    m = z.shape[0]
    width = z.shape[1] // 2
    tm = chunks * A_CHUNK
    return pl.pallas_call(
        functools.partial(_spatial_gate_kernel, width=width, chunks=chunks),
        grid=(m // tm,),
        in_specs=[pl.BlockSpec((tm, 2 * width), lambda i: (i, 0)),
                  pl.BlockSpec((1, width), lambda i: (0, 0)),
                  pl.BlockSpec((1, width), lambda i: (0, 0)),
                  pl.BlockSpec((A_GROUPS, A_CHUNK, A_CHUNK), lambda i: (0, 0, 0)),
                  pl.BlockSpec((A_CHUNK, A_GROUPS), lambda i: (0, 0))],
        out_specs=pl.BlockSpec((tm, width), lambda i: (i, 0)),
        out_shape=jax.ShapeDtypeStruct((m, width), BF16),
        compiler_params=_params("parallel"),
        name="spatial_gate",
    )(z, ln_g.reshape(1, width), ln_b.reshape(1, width), w_s, b_s_t)


def _gla_kernel(*refs, reverse, n_chunks):
    if reverse:
        (q_ref, k_ref, v_ref, low_ref, wgk_ref, bias_ref,
         ofwd_ref, gate_ref, ng_ref, o_ref, state_ref) = refs
    else:
        q_ref, k_ref, v_ref, low_ref, wgk_ref, bias_ref, o_ref, state_ref = refs
    dk = q_ref.shape[-1]
    c_len = B_CHUNK

    @pl.when(pl.program_id(2) == 0)
    def _():
        state_ref[...] = jnp.zeros_like(state_ref)

    row = lax.broadcasted_iota(jnp.int32, (c_len, c_len), 0)
    col = lax.broadcasted_iota(jnp.int32, (c_len, c_len), 1)
    if reverse:
        cum = (col >= row).astype(F32)
        keep = col > row
        mid, last = c_len // 2, 0
    else:
        cum = (col <= row).astype(F32)
        keep = col <= row
        mid, last = c_len // 2 - 1, c_len - 1

    pre = jnp.dot(low_ref[0].astype(BF16), wgk_ref[...], preferred_element_type=F32) + bias_ref[...]
    log_a_all = jax.nn.log_sigmoid(pre) / B_GATE_NORMALIZER
    state = state_ref[...]

    for ci in range(n_chunks):
        c = (n_chunks - 1 - ci) if reverse else ci
        rows = slice(c * c_len, (c + 1) * c_len)
        q = q_ref[0, rows, :] * (dk ** -0.5)
        k = k_ref[0, rows, :]
        v = v_ref[0, rows, :].astype(BF16)
        log_a = log_a_all[rows, :]
        b = jnp.dot(cum, log_a, preferred_element_type=F32, precision=lax.Precision.HIGHEST)
        b_mid = b[mid:mid + 1, :]
        b_last = b[last:last + 1, :]
        q_rel = (q * jnp.exp(b - b_mid)).astype(BF16)
        k_rel = (k * jnp.exp(b_mid - b)).astype(BF16)
        scores = lax.dot_general(q_rel, k_rel, (((1,), (1,)), ((), ())),
                                 preferred_element_type=F32)
        scores = jnp.where(keep, scores, 0.0).astype(BF16)
        o = jnp.dot(scores, v, preferred_element_type=F32)
        q_inter = (q * jnp.exp(b)).astype(BF16)
        k_upd = (k * jnp.exp(b_last - b)).astype(BF16)
        o = o + lax.dot_general(q_inter, state.astype(BF16), (((1,), (1,)), ((), ())),
                                preferred_element_type=F32)
        state = state * jnp.exp(b_last) + lax.dot_general(
            v, k_upd, (((0,), (0,)), ((), ())), preferred_element_type=F32)
        if reverse:
            o = o + ofwd_ref[0, rows, :]
            o = o * lax.rsqrt(jnp.mean(o * o, axis=-1, keepdims=True) + RMS_EPS) * ng_ref[...]
            o = o * jax.nn.silu(gate_ref[0, rows, :])
        o_ref[0, rows, :] = o.astype(o_ref.dtype)

    state_ref[...] = state


def _gla_direction(proj, wgk, bias, *, reverse, blk, o_fwd=None, norm_g=None):
    bsz, seq, _ = proj.shape
    dk = wgk.shape[1] // B_HEADS
    dv = 2 * dk
    key_dim = dk * B_HEADS
    val_dim = dv * B_HEADS
    blk = min(blk, seq)
    nb = seq // blk
    low_block = (2 * key_dim + 2 * val_dim) // LANE

    def sblk(n):
        return (nb - 1 - n) if reverse else n

    in_specs = [
        pl.BlockSpec((1, blk, dk), lambda b, h, n: (b, sblk(n), h)),
        pl.BlockSpec((1, blk, dk), lambda b, h, n: (b, sblk(n), key_dim // dk + h)),
        pl.BlockSpec((1, blk, dv), lambda b, h, n: (b, sblk(n), 2 * key_dim // dv + h)),
        pl.BlockSpec((1, blk, LANE), lambda b, h, n: (b, sblk(n), low_block)),
        pl.BlockSpec((LANE, dk), lambda b, h, n: (0, h)),
        pl.BlockSpec((1, dk), lambda b, h, n: (0, h)),
    ]
    args = [proj, proj, proj, proj, wgk, bias.reshape(1, key_dim)]
    if reverse:
        in_specs += [
            pl.BlockSpec((1, blk, dv), lambda b, h, n: (b, sblk(n), h)),
            pl.BlockSpec((1, blk, dv), lambda b, h, n: (b, sblk(n), (2 * key_dim + val_dim) // dv + h)),
            pl.BlockSpec((1, dv), lambda b, h, n: (0, 0)),
        ]
        args += [o_fwd, proj, norm_g.reshape(1, dv)]
    return pl.pallas_call(
        functools.partial(_gla_kernel, reverse=reverse, n_chunks=blk // B_CHUNK),
        grid=(bsz, B_HEADS, nb),
        in_specs=in_specs,
        out_specs=pl.BlockSpec((1, blk, dv), lambda b, h, n: (b, sblk(n), h)),
        out_shape=jax.ShapeDtypeStruct((bsz, seq, val_dim), BF16 if reverse else F32),
        scratch_shapes=[pltpu.VMEM((dv, dk), F32)],
        compiler_params=_params("parallel", "parallel", "arbitrary"),
        name="gla_bwd" if reverse else "gla_fwd",
    )(*args)


def _ffn_kernel(te_ref, tv_ref, *refs, fuse_ln):
    if fuse_ln:
        x_ref, w1_ref, w3_ref, w2_ref, r_ref, g_ref, b_ref, o_ref, o16_ref, acc_ref = refs
    else:
        x_ref, w1_ref, w3_ref, w2_ref, o_ref = refs
        acc_ref = o_ref
    i = pl.program_id(0)
    j = pl.program_id(1)

    @pl.when(j == 0)
    def _():
        acc_ref[...] = jnp.zeros_like(acc_ref)

    @pl.when(tv_ref[i] != 0)
    def _():
        x = x_ref[...]
        a = jnp.dot(x, w1_ref[...], preferred_element_type=F32)
        c = jnp.dot(x, w3_ref[...], preferred_element_type=F32)
        h = (jax.nn.silu(a) * c).astype(BF16)
        acc_ref[...] += jnp.dot(h, w2_ref[...], preferred_element_type=F32)

    if fuse_ln:
        @pl.when(j == pl.num_programs(1) - 1)
        def _():
            y = _layer_norm(DEEPNORM_ALPHA * r_ref[...] + acc_ref[...], g_ref[...], b_ref[...])
            o_ref[...] = y
            o16_ref[...] = y.astype(BF16)


def _ffn(xs, w1, w3, w2, tile_expert, tile_valid, *, tm, tf, ln=None):
    rows, d = xs.shape
    f = w1.shape[2]
    nj = f // tf
    fuse_ln = ln is not None

    def wcol(i, j, te, tv):
        return jnp.where(tv[i] != 0, j, nj - 1)

    in_specs = [
        pl.BlockSpec((tm, d), lambda i, j, te, tv: (i, 0)),
        pl.BlockSpec((None, d, tf), lambda i, j, te, tv: (te[i], 0, wcol(i, j, te, tv))),
        pl.BlockSpec((None, d, tf), lambda i, j, te, tv: (te[i], 0, wcol(i, j, te, tv))),
        pl.BlockSpec((None, tf, d), lambda i, j, te, tv: (te[i], wcol(i, j, te, tv), 0)),
    ]
    args = [xs, w1, w3, w2]
    out_specs = [pl.BlockSpec((tm, d), lambda i, j, te, tv: (i, 0))]
    out_shape = [jax.ShapeDtypeStruct((rows, d), F32)]
    if fuse_ln:
        resid, g, b = ln
        in_specs += [pl.BlockSpec((tm, d), lambda i, j, te, tv: (i, 0)),
                     pl.BlockSpec((1, d), lambda i, j, te, tv: (0, 0)),
                     pl.BlockSpec((1, d), lambda i, j, te, tv: (0, 0))]
        args += [resid, g.reshape(1, d), b.reshape(1, d)]
        out_specs.append(pl.BlockSpec((tm, d), lambda i, j, te, tv: (i, 0)))
        out_shape.append(jax.ShapeDtypeStruct((rows, d), BF16))
    out = pl.pallas_call(
        functools.partial(_ffn_kernel, fuse_ln=fuse_ln),
        grid_spec=pltpu.PrefetchScalarGridSpec(
            num_scalar_prefetch=2,
            grid=(rows // tm, nj),
            in_specs=in_specs,
            out_specs=out_specs,
            scratch_shapes=[pltpu.VMEM((tm, d), F32)] if fuse_ln else []),
        out_shape=out_shape,
        compiler_params=_params("parallel", "arbitrary"),
        name="ffn_ln" if fuse_ln else "ffn_experts",
    )(tile_expert, tile_valid, *args)
    return out if fuse_ln else out[0]


def _router_kernel(x_ref, w_ref, gate_ref, idx_ref):
    logits = jnp.dot(x_ref[...], w_ref[...], preferred_element_type=F32)
    col = lax.broadcasted_iota(jnp.int32, logits.shape, 1)
    neg = jnp.float32(-jnp.inf)
    logits = jnp.where(col < N_EXPERTS, logits, neg)
    m1 = jnp.max(logits, axis=-1, keepdims=True)
    i1 = jnp.min(jnp.where(logits == m1, col, LANE), axis=-1, keepdims=True)
    rest = jnp.where(col == i1, neg, logits)
    m2 = jnp.max(rest, axis=-1, keepdims=True)
    i2 = jnp.min(jnp.where(rest == m2, col, LANE), axis=-1, keepdims=True)
    e1 = jnp.exp(m1 - m1)
    e2 = jnp.exp(m2 - m1)
    total = e1 + e2
    gate_ref[...] = jnp.where(col == 0, e1 / total, jnp.where(col == 1, e2 / total, 0.0))
    idx_ref[...] = jnp.where(col == 0, i1, jnp.where(col == 1, i2, 0))


def _router(x16, w_pad, *, tm):
    m, d = x16.shape
    tm = min(tm, m)
    return pl.pallas_call(
        _router_kernel,
        grid=(m // tm,),
        in_specs=[pl.BlockSpec((tm, d), lambda i: (i, 0)),
                  pl.BlockSpec((d, LANE), lambda i: (0, 0))],
        out_specs=[pl.BlockSpec((tm, LANE), lambda i: (i, 0)),
                   pl.BlockSpec((tm, LANE), lambda i: (i, 0))],
        out_shape=[jax.ShapeDtypeStruct((m, LANE), F32),
                   jax.ShapeDtypeStruct((m, LANE), jnp.int32)],
        compiler_params=_params("parallel"),
        name="router",
    )(x16, w_pad)


def _combine_ln_kernel(r_ref, y0_ref, y1_ref, gate_ref, g_ref, b_ref, o_ref, o16_ref):
    gates = gate_ref[...]
    ff = gates[:, 0:1] * y0_ref[...] + gates[:, 1:2] * y1_ref[...]
    y = _layer_norm(DEEPNORM_ALPHA * r_ref[...] + ff, g_ref[...], b_ref[...])
    o_ref[...] = y
    o16_ref[...] = y.astype(BF16)


def _combine_ln(resid, y0, y1, gates, g, b, *, tm):
    m, d = resid.shape
    tm = min(tm, m)
    row = pl.BlockSpec((tm, d), lambda i: (i, 0))
    vec = pl.BlockSpec((1, d), lambda i: (0, 0))
    return pl.pallas_call(
        _combine_ln_kernel,
        grid=(m // tm,),
        in_specs=[row, row, row, pl.BlockSpec((tm, LANE), lambda i: (i, 0)), vec, vec],
        out_specs=[row, row],
        out_shape=[jax.ShapeDtypeStruct((m, d), F32), jax.ShapeDtypeStruct((m, d), BF16)],
        compiler_params=_params("parallel"),
        name="combine_ln",
    )(resid, y0, y1, gates, g.reshape(1, d), b.reshape(1, d))


FFN_TILE_F = 512
GLA_PROJ_TILES = 5


def _cast_cols_kernel(w_ref, o_ref):
    n = w_ref.shape[1]
    o_ref[:, :n] = w_ref[...].astype(BF16)
    if o_ref.shape[1] > n:
        o_ref[:, n:] = jnp.zeros((o_ref.shape[0], o_ref.shape[1] - n), BF16)


def _bf16_cols_padded(w, n_out, *, tr=256):
    lead, n = w.shape[:-1], w.shape[-1]
    w2d = w.reshape(-1, n)
    rows = w2d.shape[0]
    out = pl.pallas_call(
        _cast_cols_kernel,
        grid=(rows // tr,),
        in_specs=[pl.BlockSpec((tr, n), lambda i: (i, 0))],
        out_specs=pl.BlockSpec((tr, n_out), lambda i: (i, 0)),
        out_shape=jax.ShapeDtypeStruct((rows, n_out), BF16),
        compiler_params=_params("parallel"),
        name="cast_cols",
    )(w2d)
    return out.reshape(*lead, n_out)


def _bf16_padded(w, axis, size):
    pad = [(0, 0)] * w.ndim
    pad[axis % w.ndim] = (0, size - w.shape[axis])
    return jnp.pad(w.astype(BF16), pad)


def _gla_proj_cols(w_in):
    main = w_in.shape[-1] - 2 * B_GATE_RANK
    return _round_up(main + LANE, GLA_PROJ_TILES * MXU_DIM)


def _gmlp_layer(x, x16, layer, w_in, v_ln_g, v_ln_b, w_s, b_s, w_out, ln_g, ln_b):
    z = _mm_act(x16, w_in, layer, gelu=True, out_dtype=BF16, tm=1024, tn=1024)
    h = _spatial_gate(z, v_ln_g, v_ln_b, w_s.astype(BF16), b_s.T, chunks=2)
    return _mm_ln(h, w_out, layer, x, ln_g, ln_b, tm=512, tk=1024)


def _gla_layer(x, x16, bsz, layer, w_in, w_gk_f, bias_f, w_gk_b, bias_b, norm_g, w_out, ln_g, ln_b):
    t, d = x.shape
    seq = t // bsz
    key_dim = w_gk_f.shape[1]
    cols = w_in.shape[2]
    proj = _mm_act(x16, w_in, layer, gelu=False, out_dtype=F32, tm=1024, tn=cols // GLA_PROJ_TILES)
    proj = proj.reshape(bsz, seq, cols)
    wf = jnp.zeros((LANE, key_dim), BF16).at[:B_GATE_RANK].set(w_gk_f.astype(BF16))
    wb = jnp.zeros((LANE, key_dim), BF16).at[B_GATE_RANK:2 * B_GATE_RANK].set(w_gk_b.astype(BF16))
    o_fwd = _gla_direction(proj, wf, bias_f, reverse=False, blk=512)
    h = _gla_direction(proj, wb, bias_b, reverse=True, blk=512, o_fwd=o_fwd, norm_g=norm_g)
    return _mm_ln(h.reshape(t, -1), w_out, layer, x, ln_g, ln_b, tm=512, tk=1024)


def _dense_ffn_layer(x, x16, layer, w1, w3, w2, ln_g, ln_b, *, tm):
    t = x.shape[0]
    tm = min(tm, t)
    n_tiles = t // tm
    return _ffn(x16, w1, w3, w2, jnp.full((n_tiles,), layer, jnp.int32), jnp.ones((n_tiles,), jnp.int32),
                tm=tm, tf=FFN_TILE_F, ln=(x, ln_g, ln_b))


def _moe_layer(x, x16, layer, router, w1, w3, w2, ln_g, ln_b, *, tm):
    t, d = x.shape
    tm = min(tm, t)
    router_p = jnp.pad(router.astype(BF16), ((0, 0), (0, LANE - N_EXPERTS)))
    gates, idx = _router(x16, router_p, tm=1024)

    pair_expert = idx[:, :TOP_K].reshape(-1)
    onehot = (pair_expert[:, None] == jnp.arange(N_EXPERTS, dtype=jnp.int32)[None, :]).astype(jnp.int32)
    rank = jnp.take_along_axis(jnp.cumsum(onehot, axis=0) - onehot, pair_expert[:, None], axis=1)[:, 0]
    counts = jnp.sum(onehot, axis=0)
    padded = (counts + tm - 1) // tm * tm
    ends = jnp.cumsum(padded)
    starts = ends - padded
    pair_row = starts[pair_expert] + rank
    n_tiles = (t * TOP_K) // tm + N_EXPERTS
    rows = n_tiles * tm
    row_token = jnp.zeros((rows,), jnp.int32).at[pair_row].set(
        jnp.arange(t * TOP_K, dtype=jnp.int32) // TOP_K)
    tile_start = jnp.arange(n_tiles, dtype=jnp.int32) * tm
    tile_valid = (tile_start < ends[-1]).astype(jnp.int32)
    tile_expert = jnp.minimum(jnp.sum((tile_start[:, None] >= ends[None, :]).astype(jnp.int32), axis=1),
                              N_EXPERTS - 1)
    last_expert = jnp.max(jnp.where(counts > 0, jnp.arange(N_EXPERTS, dtype=jnp.int32), 0))
    tile_expert = jnp.where(tile_valid != 0, tile_expert, last_expert)

    xs = x16.at[row_token].get(mode="promise_in_bounds")
    ys = _ffn(xs, w1, w3, w2, tile_expert + layer * N_EXPERTS, tile_valid, tm=tm, tf=FFN_TILE_F)
    pos = pair_row.reshape(t, TOP_K)
    y0 = ys.at[pos[:, 0]].get(mode="promise_in_bounds")
    y1 = ys.at[pos[:, 1]].get(mode="promise_in_bounds")
    return _combine_ln(x, y0, y1, gates, ln_g, ln_b, tm=512)


def kernel(x, a_w_in, a_v_ln_g, a_v_ln_b, a_w_s, a_b_s, a_w_out, b_w_in, b_w_gk_f, b_gk_bias_f, b_w_gk_b, b_gk_bias_b, b_norm_g, b_w_out, ffn_w1, ffn_w3, ffn_w2, moe_router, moe_w1, moe_w3, moe_w2, ln_g, ln_b):
    bsz, seq, d = x.shape
    x = x.reshape(bsz * seq, d)
    x16 = x.astype(BF16)

    f_pad = _round_up(ffn_w1.shape[-1], FFN_TILE_F)
    a_w_in = a_w_in.astype(BF16)
    a_w_out = a_w_out.astype(BF16)
    b_w_in = _bf16_padded(b_w_in, -1, _gla_proj_cols(b_w_in))
    b_w_out = b_w_out.astype(BF16)
    ffn_w1 = _bf16_cols_padded(ffn_w1, f_pad)
    ffn_w3 = _bf16_cols_padded(ffn_w3, f_pad)
    ffn_w2 = _bf16_padded(ffn_w2, -2, f_pad)
    moe_w1 = _bf16_cols_padded(moe_w1, f_pad).reshape(-1, d, f_pad)
    moe_w3 = _bf16_cols_padded(moe_w3, f_pad).reshape(-1, d, f_pad)
    moe_w2 = _bf16_padded(moe_w2, -2, f_pad).reshape(-1, f_pad, d)

    for i in range(DEPTH):
        j = i // 2
        if i % 2 == 0:
            x, x16 = _gmlp_layer(x, x16, j, a_w_in, a_v_ln_g[j], a_v_ln_b[j], a_w_s[j], a_b_s[j],
                                 a_w_out, ln_g[i, 0], ln_b[i, 0])
            x, x16 = _dense_ffn_layer(x, x16, j, ffn_w1, ffn_w3, ffn_w2, ln_g[i, 1], ln_b[i, 1], tm=512)
        else:
            x, x16 = _gla_layer(x, x16, bsz, j, b_w_in, b_w_gk_f[j], b_gk_bias_f[j], b_w_gk_b[j],
                                b_gk_bias_b[j], b_norm_g[j], b_w_out, ln_g[i, 0], ln_b[i, 0])
            x, x16 = _moe_layer(x, x16, j, moe_router[j], moe_w1, moe_w3, moe_w2,
                                ln_g[i, 1], ln_b[i, 1], tm=512)
    return x.reshape(bsz, seq, d)
```

```python
import functools

import jax
import jax.numpy as jnp
from jax import lax
from jax.experimental import pallas as pl
from jax.experimental.pallas import tpu as pltpu

F32 = jnp.float32
BF16 = jnp.bfloat16

DEPTH = 4
A_CHUNK = 128
A_GROUPS = 8
B_HEADS = 4
B_GATE_RANK = 16
B_GATE_NORMALIZER = 16.0
B_CHUNK = 64
N_EXPERTS = 8
TOP_K = 2
DEEPNORM_ALPHA = (2.0 * DEPTH) ** 0.25
LN_EPS = 1e-5
RMS_EPS = 1e-5

LANE = 128
MXU_DIM = 256
VMEM_LIMIT_BYTES = 56 * 1024 * 1024


def _round_up(n, m):
    return (n + m - 1) // m * m


def _params(*semantics):
    return pltpu.CompilerParams(dimension_semantics=semantics,
                                vmem_limit_bytes=VMEM_LIMIT_BYTES)


def _layer_norm(y, g, b):
    mu = jnp.mean(y, axis=-1, keepdims=True)
    d = y - mu
    var = jnp.mean(d * d, axis=-1, keepdims=True)
    return d * lax.rsqrt(var + LN_EPS) * g + b


def _mm_act_kernel(x_ref, w_ref, o_ref, *, gelu):
    acc = jnp.dot(x_ref[...], w_ref[...], preferred_element_type=F32)
    if gelu:
        acc = 0.5 * acc * (1.0 + lax.erf(acc * (2.0 ** -0.5)))
    o_ref[...] = acc.astype(o_ref.dtype)


def _mm_act(x, w, layer, *, gelu, out_dtype, tm, tn):
    m, k = x.shape
    n = w.shape[2]
    tm = min(tm, m)
    return pl.pallas_call(
        functools.partial(_mm_act_kernel, gelu=gelu),
        grid=(m // tm, n // tn),
        in_specs=[pl.BlockSpec((tm, k), lambda i, j: (i, 0)),
                  pl.BlockSpec((None, k, tn), lambda i, j: (layer, 0, j))],
        out_specs=pl.BlockSpec((tm, tn), lambda i, j: (i, j)),
        out_shape=jax.ShapeDtypeStruct((m, n), out_dtype),
        compiler_params=_params("parallel", "arbitrary"),
        name="mm_act",
    )(x, w)


def _mm_ln_kernel(h_ref, w_ref, r_ref, g_ref, b_ref, o_ref, o16_ref, *, sub):
    for s in range(h_ref.shape[0] // sub):
        rows = slice(s * sub, (s + 1) * sub)
        acc = jnp.dot(h_ref[rows, :], w_ref[...], preferred_element_type=F32)
        y = _layer_norm(DEEPNORM_ALPHA * r_ref[rows, :] + acc, g_ref[...], b_ref[...])
        o_ref[rows, :] = y
        o16_ref[rows, :] = y.astype(BF16)


def _mm_ln(h, w, layer, resid, g, b, *, tm, sub):
    m, kdim = h.shape
    n = w.shape[2]
    tm = min(tm, m)
    return pl.pallas_call(
        functools.partial(_mm_ln_kernel, sub=sub),
        grid=(m // tm,),
        in_specs=[pl.BlockSpec((tm, kdim), lambda i: (i, 0)),
                  pl.BlockSpec((None, kdim, n), lambda i: (layer, 0, 0),
                               pipeline_mode=pl.Buffered(1)),
                  pl.BlockSpec((tm, n), lambda i: (i, 0)),
                  pl.BlockSpec((1, n), lambda i: (0, 0)),
                  pl.BlockSpec((1, n), lambda i: (0, 0))],
        out_specs=[pl.BlockSpec((tm, n), lambda i: (i, 0)),
                   pl.BlockSpec((tm, n), lambda i: (i, 0))],
        out_shape=[jax.ShapeDtypeStruct((m, n), F32),
                   jax.ShapeDtypeStruct((m, n), BF16)],
        compiler_params=_params("parallel"),
        name="mm_ln",
    )(h, w, resid, g.reshape(1, n), b.reshape(1, n))


def _spatial_gate_kernel(z_ref, g_ref, b_ref, ws_ref, bs_ref, o_ref, *, width, chunks):
    gdim = width // A_GROUPS
    v = z_ref[:, width:].astype(F32)
    vn = _layer_norm(v, g_ref[...], b_ref[...]).astype(BF16)
    for grp in range(A_GROUPS):
        w = ws_ref[grp]
        bias = bs_ref[:, grp:grp + 1]
        cols = slice(grp * gdim, (grp + 1) * gdim)
        for c in range(chunks):
            rows = slice(c * A_CHUNK, (c + 1) * A_CHUNK)
            s = jnp.dot(w, vn[rows, cols], preferred_element_type=F32) + bias
            u = z_ref[rows, cols].astype(F32)
            o_ref[rows, cols] = (u * s).astype(BF16)


def _spatial_gate(z, ln_g, ln_b, w_s, b_s_t, *, chunks):
    m = z.shape[0]
    width = z.shape[1] // 2
    tm = chunks * A_CHUNK
    return pl.pallas_call(
        functools.partial(_spatial_gate_kernel, width=width, chunks=chunks),
        grid=(m // tm,),
        in_specs=[pl.BlockSpec((tm, 2 * width), lambda i: (i, 0)),
                  pl.BlockSpec((1, width), lambda i: (0, 0)),
                  pl.BlockSpec((1, width), lambda i: (0, 0)),
                  pl.BlockSpec((A_GROUPS, A_CHUNK, A_CHUNK), lambda i: (0, 0, 0)),
                  pl.BlockSpec((A_CHUNK, A_GROUPS), lambda i: (0, 0))],
        out_specs=pl.BlockSpec((tm, width), lambda i: (i, 0)),
        out_shape=jax.ShapeDtypeStruct((m, width), BF16),
        compiler_params=_params("parallel"),
        name="spatial_gate",
    )(z, ln_g.reshape(1, width), ln_b.reshape(1, width), w_s, b_s_t)


def _gla_kernel(*refs, reverse, n_chunks):
    if reverse:
        (q_ref, k_ref, v_ref, low_ref, wgk_ref, bias_ref,
         ofwd_ref, gate_ref, ng_ref, o_ref, state_ref) = refs
    else:
        q_ref, k_ref, v_ref, low_ref, wgk_ref, bias_ref, o_ref, state_ref = refs
    dk = q_ref.shape[-1]
    c_len = B_CHUNK

    @pl.when(pl.program_id(2) == 0)
    def _():
        state_ref[...] = jnp.zeros_like(state_ref)

    row = lax.broadcasted_iota(jnp.int32, (c_len, c_len), 0)
    col = lax.broadcasted_iota(jnp.int32, (c_len, c_len), 1)
    if reverse:
        keep = col > row
        mid, last = c_len // 2, 0
    else:
        keep = col <= row
        mid, last = c_len // 2 - 1, c_len - 1

    pre = jnp.dot(low_ref[0].astype(BF16), wgk_ref[...], preferred_element_type=F32) + bias_ref[...]
    log_a_all = jax.nn.log_sigmoid(pre) / B_GATE_NORMALIZER

    n_rows = log_a_all.shape[0]
    pos = lax.broadcasted_iota(jnp.int32, log_a_all.shape, 0) % c_len
    b_all = log_a_all
    step = 1
    while step < c_len:
        if reverse:
            shifted = pltpu.roll(b_all, n_rows - step, axis=0)
            b_all = b_all + jnp.where(pos < c_len - step, shifted, 0.0)
        else:
            shifted = pltpu.roll(b_all, step, axis=0)
            b_all = b_all + jnp.where(pos >= step, shifted, 0.0)
        step *= 2

    state = state_ref[...]

    for ci in range(n_chunks):
        c = (n_chunks - 1 - ci) if reverse else ci
        rows = slice(c * c_len, (c + 1) * c_len)
        q = q_ref[0, rows, :] * (dk ** -0.5)
        k = k_ref[0, rows, :]
        v = v_ref[0, rows, :].astype(BF16)
        b = b_all[rows, :]
        b_mid = b[mid:mid + 1, :]
        b_last = b[last:last + 1, :]
        q_rel = (q * jnp.exp(b - b_mid)).astype(BF16)
        k_rel = (k * jnp.exp(b_mid - b)).astype(BF16)
        scores = lax.dot_general(q_rel, k_rel, (((1,), (1,)), ((), ())),
                                 preferred_element_type=F32)
        scores = jnp.where(keep, scores, 0.0).astype(BF16)
        o = jnp.dot(scores, v, preferred_element_type=F32)
        q_inter = (q * jnp.exp(b)).astype(BF16)
        k_upd = (k * jnp.exp(b_last - b)).astype(BF16)
        o = o + lax.dot_general(q_inter, state.astype(BF16), (((1,), (1,)), ((), ())),
                                preferred_element_type=F32)
        state = state * jnp.exp(b_last) + lax.dot_general(
            v, k_upd, (((0,), (0,)), ((), ())), preferred_element_type=F32)
        if reverse:
            o = o + ofwd_ref[0, rows, :]
            o = o * lax.rsqrt(jnp.mean(o * o, axis=-1, keepdims=True) + RMS_EPS) * ng_ref[...]
            o = o * jax.nn.silu(gate_ref[0, rows, :])
        o_ref[0, rows, :] = o.astype(o_ref.dtype)

    state_ref[...] = state


def _gla_direction(proj, wgk, bias, *, reverse, blk, o_fwd=None, norm_g=None):
    bsz, seq, _ = proj.shape
    dk = wgk.shape[1] // B_HEADS
    dv = 2 * dk
    key_dim = dk * B_HEADS
    val_dim = dv * B_HEADS
    blk = min(blk, seq)
    nb = seq // blk
    low_block = (2 * key_dim + 2 * val_dim) // LANE

    def sblk(n):
        return (nb - 1 - n) if reverse else n

    in_specs = [
        pl.BlockSpec((1, blk, dk), lambda b, h, n: (b, sblk(n), h)),
        pl.BlockSpec((1, blk, dk), lambda b, h, n: (b, sblk(n), key_dim // dk + h)),
        pl.BlockSpec((1, blk, dv), lambda b, h, n: (b, sblk(n), 2 * key_dim // dv + h)),
        pl.BlockSpec((1, blk, LANE), lambda b, h, n: (b, sblk(n), low_block)),
        pl.BlockSpec((LANE, dk), lambda b, h, n: (0, h)),
        pl.BlockSpec((1, dk), lambda b, h, n: (0, h)),
    ]
    args = [proj, proj, proj, proj, wgk, bias.reshape(1, key_dim)]
    if reverse:
        in_specs += [
            pl.BlockSpec((1, blk, dv), lambda b, h, n: (b, sblk(n), h)),
            pl.BlockSpec((1, blk, dv), lambda b, h, n: (b, sblk(n), (2 * key_dim + val_dim) // dv + h)),
            pl.BlockSpec((1, dv), lambda b, h, n: (0, 0)),
        ]
        args += [o_fwd, proj, norm_g.reshape(1, dv)]
    return pl.pallas_call(
        functools.partial(_gla_kernel, reverse=reverse, n_chunks=blk // B_CHUNK),
        grid=(bsz, B_HEADS, nb),
        in_specs=in_specs,
        out_specs=pl.BlockSpec((1, blk, dv), lambda b, h, n: (b, sblk(n), h)),
        out_shape=jax.ShapeDtypeStruct((bsz, seq, val_dim), BF16 if reverse else F32),
        scratch_shapes=[pltpu.VMEM((dv, dk), F32)],
        compiler_params=_params("parallel", "parallel", "arbitrary"),
        name="gla_bwd" if reverse else "gla_fwd",
    )(*args)


def _ffn_kernel(te_ref, tv_ref, *refs, fuse_ln):
    if fuse_ln:
        x_ref, w1_ref, w3_ref, w2_ref, r_ref, g_ref, b_ref, o_ref, o16_ref, acc_ref = refs
    else:
        x_ref, w1_ref, w3_ref, w2_ref, o_ref = refs
        acc_ref = o_ref
    i = pl.program_id(0)
    j = pl.program_id(1)

    @pl.when(j == 0)
    def _():
        acc_ref[...] = jnp.zeros_like(acc_ref)

    @pl.when(tv_ref[i] != 0)
    def _():
        x = x_ref[...]
        a = jnp.dot(x, w1_ref[...], preferred_element_type=F32)
        c = jnp.dot(x, w3_ref[...], preferred_element_type=F32)
        h = (jax.nn.silu(a) * c).astype(BF16)
        acc_ref[...] += jnp.dot(h, w2_ref[...], preferred_element_type=F32)

    if fuse_ln:
        @pl.when(j == pl.num_programs(1) - 1)
        def _():
            y = _layer_norm(DEEPNORM_ALPHA * r_ref[...] + acc_ref[...], g_ref[...], b_ref[...])
            o_ref[...] = y
            o16_ref[...] = y.astype(BF16)


def _ffn(xs, w1, w3, w2, tile_expert, tile_valid, *, tm, tf, ln=None):
    rows, d = xs.shape
    f = w1.shape[2]
    nj = f // tf
    fuse_ln = ln is not None

    def wcol(i, j, te, tv):
        return jnp.where(tv[i] != 0, j, nj - 1)

    in_specs = [
        pl.BlockSpec((tm, d), lambda i, j, te, tv: (i, 0)),
        pl.BlockSpec((None, d, tf), lambda i, j, te, tv: (te[i], 0, wcol(i, j, te, tv))),
        pl.BlockSpec((None, d, tf), lambda i, j, te, tv: (te[i], 0, wcol(i, j, te, tv))),
        pl.BlockSpec((None, tf, d), lambda i, j, te, tv: (te[i], wcol(i, j, te, tv), 0)),
    ]
    args = [xs, w1, w3, w2]
    out_specs = [pl.BlockSpec((tm, d), lambda i, j, te, tv: (i, 0))]
    out_shape = [jax.ShapeDtypeStruct((rows, d), F32)]
    if fuse_ln:
        resid, g, b = ln
        in_specs += [pl.BlockSpec((tm, d), lambda i, j, te, tv: (i, 0)),
                     pl.BlockSpec((1, d), lambda i, j, te, tv: (0, 0)),
                     pl.BlockSpec((1, d), lambda i, j, te, tv: (0, 0))]
        args += [resid, g.reshape(1, d), b.reshape(1, d)]
        out_specs.append(pl.BlockSpec((tm, d), lambda i, j, te, tv: (i, 0)))
        out_shape.append(jax.ShapeDtypeStruct((rows, d), BF16))
    out = pl.pallas_call(
        functools.partial(_ffn_kernel, fuse_ln=fuse_ln),
        grid_spec=pltpu.PrefetchScalarGridSpec(
            num_scalar_prefetch=2,
            grid=(rows // tm, nj),
            in_specs=in_specs,
            out_specs=out_specs,
            scratch_shapes=[pltpu.VMEM((tm, d), F32)] if fuse_ln else []),
        out_shape=out_shape,
        compiler_params=_params("parallel", "arbitrary"),
        name="ffn_ln" if fuse_ln else "ffn_experts",
    )(tile_expert, tile_valid, *args)
    return out if fuse_ln else out[0]


def _router_kernel(x_ref, w_ref, gate_ref, idx_ref):
    logits = jnp.dot(x_ref[...], w_ref[...], preferred_element_type=F32)
    col = lax.broadcasted_iota(jnp.int32, logits.shape, 1)
    neg = jnp.float32(-jnp.inf)
    logits = jnp.where(col < N_EXPERTS, logits, neg)
    m1 = jnp.max(logits, axis=-1, keepdims=True)
    i1 = jnp.min(jnp.where(logits == m1, col, LANE), axis=-1, keepdims=True)
    rest = jnp.where(col == i1, neg, logits)
    m2 = jnp.max(rest, axis=-1, keepdims=True)
    i2 = jnp.min(jnp.where(rest == m2, col, LANE), axis=-1, keepdims=True)
    e1 = jnp.exp(m1 - m1)
    e2 = jnp.exp(m2 - m1)
    total = e1 + e2
    gate_ref[...] = jnp.where(col == 0, e1 / total, jnp.where(col == 1, e2 / total, 0.0))
    idx_ref[...] = jnp.where(col == 0, i1, jnp.where(col == 1, i2, 0))


def _router(x16, w_pad, *, tm):
    m, d = x16.shape
    tm = min(tm, m)
    return pl.pallas_call(
        _router_kernel,
        grid=(m // tm,),
        in_specs=[pl.BlockSpec((tm, d), lambda i: (i, 0)),
                  pl.BlockSpec((d, LANE), lambda i: (0, 0))],
        out_specs=[pl.BlockSpec((tm, LANE), lambda i: (i, 0)),
                   pl.BlockSpec((tm, LANE), lambda i: (i, 0))],
        out_shape=[jax.ShapeDtypeStruct((m, LANE), F32),
                   jax.ShapeDtypeStruct((m, LANE), jnp.int32)],
        compiler_params=_params("parallel"),
        name="router",
    )(x16, w_pad)


def _combine_ln_kernel(r_ref, y0_ref, y1_ref, gate_ref, g_ref, b_ref, o_ref, o16_ref):
    gates = gate_ref[...]
    ff = gates[:, 0:1] * y0_ref[...] + gates[:, 1:2] * y1_ref[...]
    y = _layer_norm(DEEPNORM_ALPHA * r_ref[...] + ff, g_ref[...], b_ref[...])
    o_ref[...] = y
    o16_ref[...] = y.astype(BF16)


def _combine_ln(resid, y0, y1, gates, g, b, *, tm):
    m, d = resid.shape
    tm = min(tm, m)
    row = pl.BlockSpec((tm, d), lambda i: (i, 0))
    vec = pl.BlockSpec((1, d), lambda i: (0, 0))
    return pl.pallas_call(
        _combine_ln_kernel,
        grid=(m // tm,),
        in_specs=[row, row, row, pl.BlockSpec((tm, LANE), lambda i: (i, 0)), vec, vec],
        out_specs=[row, row],
        out_shape=[jax.ShapeDtypeStruct((m, d), F32), jax.ShapeDtypeStruct((m, d), BF16)],
        compiler_params=_params("parallel"),
        name="combine_ln",
    )(resid, y0, y1, gates, g.reshape(1, d), b.reshape(1, d))


FFN_TILE_F = 512
GLA_PROJ_TILES = 5


def _cast_cols_kernel(w_ref, o_ref):
    n = w_ref.shape[1]
    o_ref[:, :n] = w_ref[...].astype(BF16)
    if o_ref.shape[1] > n:
        o_ref[:, n:] = jnp.zeros((o_ref.shape[0], o_ref.shape[1] - n), BF16)


def _bf16_cols_padded(w, n_out, *, tr=256):
    lead, n = w.shape[:-1], w.shape[-1]
    w2d = w.reshape(-1, n)
    rows = w2d.shape[0]
    out = pl.pallas_call(
        _cast_cols_kernel,
        grid=(rows // tr,),
        in_specs=[pl.BlockSpec((tr, n), lambda i: (i, 0))],
        out_specs=pl.BlockSpec((tr, n_out), lambda i: (i, 0)),
        out_shape=jax.ShapeDtypeStruct((rows, n_out), BF16),
        compiler_params=_params("parallel"),
        name="cast_cols",
    )(w2d)
    return out.reshape(*lead, n_out)


def _bf16_padded(w, axis, size):
    pad = [(0, 0)] * w.ndim
    pad[axis % w.ndim] = (0, size - w.shape[axis])
    return jnp.pad(w.astype(BF16), pad)


def _gla_proj_cols(w_in):
    main = w_in.shape[-1] - 2 * B_GATE_RANK
    return _round_up(main + LANE, GLA_PROJ_TILES * MXU_DIM)


def _gmlp_layer(x, x16, layer, w_in, v_ln_g, v_ln_b, w_s, b_s, w_out, ln_g, ln_b):
    z = _mm_act(x16, w_in, layer, gelu=True, out_dtype=BF16, tm=1024, tn=1024)
    h = _spatial_gate(z, v_ln_g, v_ln_b, w_s.astype(BF16), b_s.T, chunks=2)
    return _mm_ln(h, w_out, layer, x, ln_g, ln_b, tm=512, sub=128)


def _gla_layer(x, x16, bsz, layer, w_in, w_gk_f, bias_f, w_gk_b, bias_b, norm_g, w_out, ln_g, ln_b):
    t, d = x.shape
    seq = t // bsz
    key_dim = w_gk_f.shape[1]
    cols = w_in.shape[2]
    proj = _mm_act(x16, w_in, layer, gelu=False, out_dtype=F32, tm=1024, tn=cols // GLA_PROJ_TILES)
    proj = proj.reshape(bsz, seq, cols)
    wf = jnp.zeros((LANE, key_dim), BF16).at[:B_GATE_RANK].set(w_gk_f.astype(BF16))
    wb = jnp.zeros((LANE, key_dim), BF16).at[B_GATE_RANK:2 * B_GATE_RANK].set(w_gk_b.astype(BF16))
    o_fwd = _gla_direction(proj, wf, bias_f, reverse=False, blk=512)
    h = _gla_direction(proj, wb, bias_b, reverse=True, blk=512, o_fwd=o_fwd, norm_g=norm_g)
    return _mm_ln(h.reshape(t, -1), w_out, layer, x, ln_g, ln_b, tm=512, sub=128)


def _dense_ffn_layer(x, x16, layer, w1, w3, w2, ln_g, ln_b, *, tm):
    t = x.shape[0]
    tm = min(tm, t)
    n_tiles = t // tm
    return _ffn(x16, w1, w3, w2, jnp.full((n_tiles,), layer, jnp.int32), jnp.ones((n_tiles,), jnp.int32),
                tm=tm, tf=FFN_TILE_F, ln=(x, ln_g, ln_b))


def _moe_layer(x, x16, layer, router, w1, w3, w2, ln_g, ln_b, *, tm):
    t, d = x.shape
    tm = min(tm, t)
    router_p = jnp.pad(router.astype(BF16), ((0, 0), (0, LANE - N_EXPERTS)))
    gates, idx = _router(x16, router_p, tm=1024)

    pair_expert = idx[:, :TOP_K].reshape(-1)
    onehot = (pair_expert[:, None] == jnp.arange(N_EXPERTS, dtype=jnp.int32)[None, :]).astype(jnp.int32)
    rank = jnp.take_along_axis(jnp.cumsum(onehot, axis=0) - onehot, pair_expert[:, None], axis=1)[:, 0]
    counts = jnp.sum(onehot, axis=0)
    padded = (counts + tm - 1) // tm * tm
    ends = jnp.cumsum(padded)
    starts = ends - padded
    pair_row = starts[pair_expert] + rank
    n_tiles = (t * TOP_K) // tm + N_EXPERTS
    rows = n_tiles * tm
    row_token = jnp.zeros((rows,), jnp.int32).at[pair_row].set(
        jnp.arange(t * TOP_K, dtype=jnp.int32) // TOP_K)
    tile_start = jnp.arange(n_tiles, dtype=jnp.int32) * tm
    tile_valid = (tile_start < ends[-1]).astype(jnp.int32)
    tile_expert = jnp.minimum(jnp.sum((tile_start[:, None] >= ends[None, :]).astype(jnp.int32), axis=1),
                              N_EXPERTS - 1)
    last_expert = jnp.max(jnp.where(counts > 0, jnp.arange(N_EXPERTS, dtype=jnp.int32), 0))
    tile_expert = jnp.where(tile_valid != 0, tile_expert, last_expert)

    xs = x16.at[row_token].get(mode="promise_in_bounds")
    ys = _ffn(xs, w1, w3, w2, tile_expert + layer * N_EXPERTS, tile_valid, tm=tm, tf=FFN_TILE_F)
    pos = pair_row.reshape(t, TOP_K)
    y0 = ys.at[pos[:, 0]].get(mode="promise_in_bounds")
    y1 = ys.at[pos[:, 1]].get(mode="promise_in_bounds")
    return _combine_ln(x, y0, y1, gates, ln_g, ln_b, tm=512)


def kernel(x, a_w_in, a_v_ln_g, a_v_ln_b, a_w_s, a_b_s, a_w_out, b_w_in, b_w_gk_f, b_gk_bias_f, b_w_gk_b, b_gk_bias_b, b_norm_g, b_w_out, ffn_w1, ffn_w3, ffn_w2, moe_router, moe_w1, moe_w3, moe_w2, ln_g, ln_b):
    bsz, seq, d = x.shape
    x = x.reshape(bsz * seq, d)
    x16 = x.astype(BF16)

    f_pad = _round_up(ffn_w1.shape[-1], FFN_TILE_F)
    a_w_in = a_w_in.astype(BF16)
    a_w_out = a_w_out.astype(BF16)
    b_w_in = _bf16_padded(b_w_in, -1, _gla_proj_cols(b_w_in))
    b_w_out = b_w_out.astype(BF16)
    ffn_w1 = _bf16_cols_padded(ffn_w1, f_pad)
    ffn_w3 = _bf16_cols_padded(ffn_w3, f_pad)
    ffn_w2 = _bf16_padded(ffn_w2, -2, f_pad)
    moe_w1 = _bf16_cols_padded(moe_w1, f_pad).reshape(-1, d, f_pad)
    moe_w3 = _bf16_cols_padded(moe_w3, f_pad).reshape(-1, d, f_pad)
    moe_w2 = _bf16_padded(moe_w2, -2, f_pad).reshape(-1, f_pad, d)

    for i in range(DEPTH):
        j = i // 2
        if i % 2 == 0:
            x, x16 = _gmlp_layer(x, x16, j, a_w_in, a_v_ln_g[j], a_v_ln_b[j], a_w_s[j], a_b_s[j],
                                 a_w_out, ln_g[i, 0], ln_b[i, 0])
            x, x16 = _dense_ffn_layer(x, x16, j, ffn_w1, ffn_w3, ffn_w2, ln_g[i, 1], ln_b[i, 1], tm=512)
        else:
            x, x16 = _gla_layer(x, x16, bsz, j, b_w_in, b_w_gk_f[j], b_gk_bias_f[j], b_w_gk_b[j],
                                b_gk_bias_b[j], b_norm_g[j], b_w_out, ln_g[i, 0], ln_b[i, 0])
            x, x16 = _moe_layer(x, x16, j, moe_router[j], moe_w1, moe_w3, moe_w2,
                                ln_g[i, 1], ln_b[i, 1], tm=512)
    return x.reshape(bsz, seq, d)
```

```python
import functools

import jax
import jax.numpy as jnp
from jax import lax
from jax.experimental import pallas as pl
from jax.experimental.pallas import tpu as pltpu

F32 = jnp.float32
BF16 = jnp.bfloat16

DEPTH = 4
A_CHUNK = 128
A_GROUPS = 8
B_HEADS = 4
B_GATE_RANK = 16
B_GATE_NORMALIZER = 16.0
B_CHUNK = 64
N_EXPERTS = 8
TOP_K = 2
DEEPNORM_ALPHA = (2.0 * DEPTH) ** 0.25
LN_EPS = 1e-5
RMS_EPS = 1e-5

LANE = 128
MXU_DIM = 256
VMEM_LIMIT_BYTES = 56 * 1024 * 1024


def _round_up(n, m):
    return (n + m - 1) // m * m


def _params(*semantics):
    return pltpu.CompilerParams(dimension_semantics=semantics,
                                vmem_limit_bytes=VMEM_LIMIT_BYTES)


def _layer_norm(y, g, b):
    mu = jnp.mean(y, axis=-1, keepdims=True)
    d = y - mu
    var = jnp.mean(d * d, axis=-1, keepdims=True)
    return d * lax.rsqrt(var + LN_EPS) * g + b


def _mm_act_kernel(x_ref, w_ref, o_ref, *, gelu):
    acc = jnp.dot(x_ref[...], w_ref[...], preferred_element_type=F32)
    if gelu:
        acc = 0.5 * acc * (1.0 + lax.erf(acc * (2.0 ** -0.5)))
    o_ref[...] = acc.astype(o_ref.dtype)


def _mm_act(x, w, layer, *, gelu, out_dtype, tm, tn):
    m, k = x.shape
    n = w.shape[2]
    tm = min(tm, m)
    return pl.pallas_call(
        functools.partial(_mm_act_kernel, gelu=gelu),
        grid=(m // tm, n // tn),
        in_specs=[pl.BlockSpec((tm, k), lambda i, j: (i, 0)),
                  pl.BlockSpec((None, k, tn), lambda i, j: (layer, 0, j))],
        out_specs=pl.BlockSpec((tm, tn), lambda i, j: (i, j)),
        out_shape=jax.ShapeDtypeStruct((m, n), out_dtype),
        compiler_params=_params("parallel", "arbitrary"),
        name="mm_act",
    )(x, w)


def _mm_ln_kernel(h_ref, w_ref, r_ref, g_ref, b_ref, o_ref, o16_ref, *, sub):
    for s in range(h_ref.shape[0] // sub):
        rows = slice(s * sub, (s + 1) * sub)
        acc = jnp.dot(h_ref[rows, :], w_ref[...], preferred_element_type=F32)
        y = _layer_norm(DEEPNORM_ALPHA * r_ref[rows, :] + acc, g_ref[...], b_ref[...])
        o_ref[rows, :] = y
        o16_ref[rows, :] = y.astype(BF16)


def _mm_ln(h, w, layer, resid, g, b, *, tm, sub):
    m, kdim = h.shape
    n = w.shape[2]
    tm = min(tm, m)
    return pl.pallas_call(
        functools.partial(_mm_ln_kernel, sub=sub),
        grid=(m // tm,),
        in_specs=[pl.BlockSpec((tm, kdim), lambda i: (i, 0)),
                  pl.BlockSpec((None, kdim, n), lambda i: (layer, 0, 0),
                               pipeline_mode=pl.Buffered(1)),
                  pl.BlockSpec((tm, n), lambda i: (i, 0)),
                  pl.BlockSpec((1, n), lambda i: (0, 0)),
                  pl.BlockSpec((1, n), lambda i: (0, 0))],
        out_specs=[pl.BlockSpec((tm, n), lambda i: (i, 0)),
                   pl.BlockSpec((tm, n), lambda i: (i, 0))],
        out_shape=[jax.ShapeDtypeStruct((m, n), F32),
                   jax.ShapeDtypeStruct((m, n), BF16)],
        compiler_params=_params("parallel"),
        name="mm_ln",
    )(h, w, resid, g.reshape(1, n), b.reshape(1, n))


def _spatial_gate_kernel(z_ref, g_ref, b_ref, ws_ref, bs_ref, o_ref, *, width, chunks):
    gdim = width // A_GROUPS
    v = z_ref[:, width:].astype(F32)
    vn = _layer_norm(v, g_ref[...], b_ref[...]).astype(BF16)
    for grp in range(A_GROUPS):
        w = ws_ref[grp]
        bias = bs_ref[:, grp:grp + 1]
        cols = slice(grp * gdim, (grp + 1) * gdim)
        for c in range(chunks):
            rows = slice(c * A_CHUNK, (c + 1) * A_CHUNK)
            s = jnp.dot(w, vn[rows, cols], preferred_element_type=F32) + bias
            u = z_ref[rows, cols].astype(F32)
            o_ref[rows, cols] = (u * s).astype(BF16)


def _spatial_gate(z, ln_g, ln_b, w_s, b_s_t, *, chunks):
    m = z.shape[0]
    width = z.shape[1] // 2
    tm = chunks * A_CHUNK
    return pl.pallas_call(
        functools.partial(_spatial_gate_kernel, width=width, chunks=chunks),
        grid=(m // tm,),
        in_specs=[pl.BlockSpec((tm, 2 * width), lambda i: (i, 0)),
                  pl.BlockSpec((1, width), lambda i: (0, 0)),
                  pl.BlockSpec((1, width), lambda i: (0, 0)),
                  pl.BlockSpec((A_GROUPS, A_CHUNK, A_CHUNK), lambda i: (0, 0, 0)),
                  pl.BlockSpec((A_CHUNK, A_GROUPS), lambda i: (0, 0))],
        out_specs=pl.BlockSpec((tm, width), lambda i: (i, 0)),
        out_shape=jax.ShapeDtypeStruct((m, width), BF16),
        compiler_params=_params("parallel"),
        name="spatial_gate",
    )(z, ln_g.reshape(1, width), ln_b.reshape(1, width), w_s, b_s_t)


def _gla_kernel(*refs, reverse, n_chunks):
    if reverse:
        (q_ref, k_ref, v_ref, low_ref, wgk_ref, bias_ref,
         ofwd_ref, gate_ref, ng_ref, o_ref, state_ref) = refs
    else:
        q_ref, k_ref, v_ref, low_ref, wgk_ref, bias_ref, o_ref, state_ref = refs
    dk = q_ref.shape[-1]
    c_len = B_CHUNK

    @pl.when(pl.program_id(2) == 0)
    def _():
        state_ref[...] = jnp.zeros_like(state_ref)

    row = lax.broadcasted_iota(jnp.int32, (c_len, c_len), 0)
    col = lax.broadcasted_iota(jnp.int32, (c_len, c_len), 1)
    if reverse:
        keep = col > row
        mid, last = c_len // 2, 0
    else:
        keep = col <= row
        mid, last = c_len // 2 - 1, c_len - 1

    pre = jnp.dot(low_ref[0].astype(BF16), wgk_ref[...], preferred_element_type=F32) + bias_ref[...]
    log_a_all = jax.nn.log_sigmoid(pre) / B_GATE_NORMALIZER

    n_rows = log_a_all.shape[0]
    pos = lax.broadcasted_iota(jnp.int32, log_a_all.shape, 0) % c_len
    b_all = log_a_all
    step = 1
    while step < c_len:
        if reverse:
            shifted = pltpu.roll(b_all, n_rows - step, axis=0)
            b_all = b_all + jnp.where(pos < c_len - step, shifted, 0.0)
        else:
            shifted = pltpu.roll(b_all, step, axis=0)
            b_all = b_all + jnp.where(pos >= step, shifted, 0.0)
        step *= 2

    state = state_ref[...]

    for ci in range(n_chunks):
        c = (n_chunks - 1 - ci) if reverse else ci
        rows = slice(c * c_len, (c + 1) * c_len)
        q = q_ref[0, rows, :] * (dk ** -0.5)
        k = k_ref[0, rows, :]
        v = v_ref[0, rows, :].astype(BF16)
        b = b_all[rows, :]
        b_mid = b[mid:mid + 1, :]
        b_last = b[last:last + 1, :]
        q_rel = (q * jnp.exp(b - b_mid)).astype(BF16)
        k_rel = (k * jnp.exp(b_mid - b)).astype(BF16)
        scores = lax.dot_general(q_rel, k_rel, (((1,), (1,)), ((), ())),
                                 preferred_element_type=F32)
        scores = jnp.where(keep, scores, 0.0).astype(BF16)
        o = jnp.dot(scores, v, preferred_element_type=F32)
        q_inter = (q * jnp.exp(b)).astype(BF16)
        k_upd = (k * jnp.exp(b_last - b)).astype(BF16)
        o = o + lax.dot_general(q_inter, state.astype(BF16), (((1,), (1,)), ((), ())),
                                preferred_element_type=F32)
        state = state * jnp.exp(b_last) + lax.dot_general(
            v, k_upd, (((0,), (0,)), ((), ())), preferred_element_type=F32)
        if reverse:
            o = o + ofwd_ref[0, rows, :]
            o = o * lax.rsqrt(jnp.mean(o * o, axis=-1, keepdims=True) + RMS_EPS) * ng_ref[...]
            o = o * jax.nn.silu(gate_ref[0, rows, :])
        o_ref[0, rows, :] = o.astype(o_ref.dtype)

    state_ref[...] = state


def _gla_direction(proj, wgk, bias, *, reverse, blk, o_fwd=None, norm_g=None):
    bsz, seq, _ = proj.shape
    dk = wgk.shape[1] // B_HEADS
    dv = 2 * dk
    key_dim = dk * B_HEADS
    val_dim = dv * B_HEADS
    blk = min(blk, seq)
    nb = seq // blk
    low_block = (2 * key_dim + 2 * val_dim) // LANE

    def sblk(n):
        return (nb - 1 - n) if reverse else n

    in_specs = [
        pl.BlockSpec((1, blk, dk), lambda b, h, n: (b, sblk(n), h)),
        pl.BlockSpec((1, blk, dk), lambda b, h, n: (b, sblk(n), key_dim // dk + h)),
        pl.BlockSpec((1, blk, dv), lambda b, h, n: (b, sblk(n), 2 * key_dim // dv + h)),
        pl.BlockSpec((1, blk, LANE), lambda b, h, n: (b, sblk(n), low_block)),
        pl.BlockSpec((LANE, dk), lambda b, h, n: (0, h)),
        pl.BlockSpec((1, dk), lambda b, h, n: (0, h)),
    ]
    args = [proj, proj, proj, proj, wgk, bias.reshape(1, key_dim)]
    if reverse:
        in_specs += [
            pl.BlockSpec((1, blk, dv), lambda b, h, n: (b, sblk(n), h)),
            pl.BlockSpec((1, blk, dv), lambda b, h, n: (b, sblk(n), (2 * key_dim + val_dim) // dv + h)),
            pl.BlockSpec((1, dv), lambda b, h, n: (0, 0)),
        ]
        args += [o_fwd, proj, norm_g.reshape(1, dv)]
    return pl.pallas_call(
        functools.partial(_gla_kernel, reverse=reverse, n_chunks=blk // B_CHUNK),
        grid=(bsz, B_HEADS, nb),
        in_specs=in_specs,
        out_specs=pl.BlockSpec((1, blk, dv), lambda b, h, n: (b, sblk(n), h)),
        out_shape=jax.ShapeDtypeStruct((bsz, seq, val_dim), BF16 if reverse else F32),
        scratch_shapes=[pltpu.VMEM((dv, dk), F32)],
        compiler_params=_params("parallel", "parallel", "arbitrary"),
        name="gla_bwd" if reverse else "gla_fwd",
    )(*args)


def _ffn_kernel(te_ref, tv_ref, *refs, fuse_ln, sub):
    if fuse_ln:
        x_ref, w1_ref, w3_ref, w2_ref, r_ref, g_ref, b_ref, o_ref, o16_ref, acc_ref = refs
    else:
        x_ref, w1_ref, w3_ref, w2_ref, o_ref = refs
        acc_ref = o_ref
    i = pl.program_id(0)
    j = pl.program_id(1)
    last = pl.num_programs(1) - 1
    valid = tv_ref[i] != 0

    @pl.when(j == 0)
    def _():
        acc_ref[...] = jnp.zeros_like(acc_ref)

    def step(matmul, finish):
        n_rows = x_ref.shape[0]
        size = sub if finish else n_rows
        for s in range(n_rows // size):
            rows = slice(s * size, (s + 1) * size)
            acc = acc_ref[rows, :]
            if matmul:
                x = x_ref[rows, :]
                a = jnp.dot(x, w1_ref[...], preferred_element_type=F32)
                c = jnp.dot(x, w3_ref[...], preferred_element_type=F32)
                h = (jax.nn.silu(a) * c).astype(BF16)
                acc = acc + jnp.dot(h, w2_ref[...], preferred_element_type=F32)
            if finish:
                y = _layer_norm(DEEPNORM_ALPHA * r_ref[rows, :] + acc, g_ref[...], b_ref[...])
                o_ref[rows, :] = y
                o16_ref[rows, :] = y.astype(BF16)
            else:
                acc_ref[rows, :] = acc

    if fuse_ln:
        pl.when(jnp.logical_and(valid, j < last))(functools.partial(step, True, False))
        pl.when(jnp.logical_and(valid, j == last))(functools.partial(step, True, True))
        pl.when(jnp.logical_and(jnp.logical_not(valid), j == last))(functools.partial(step, False, True))
    else:
        pl.when(valid)(functools.partial(step, True, False))


def _ffn(xs, w1, w3, w2, tile_expert, tile_valid, *, tm, tf, w2_base=0, ln=None):
    rows, d = xs.shape
    f = w1.shape[2]
    nj = f // tf
    fuse_ln = ln is not None

    def wcol(i, j, te, tv):
        return jnp.where(tv[i] != 0, j, nj - 1)

    in_specs = [
        pl.BlockSpec((tm, d), lambda i, j, te, tv: (i, 0)),
        pl.BlockSpec((None, d, tf), lambda i, j, te, tv: (te[i], 0, wcol(i, j, te, tv))),
        pl.BlockSpec((None, d, tf), lambda i, j, te, tv: (te[i], 0, wcol(i, j, te, tv))),
        pl.BlockSpec((None, tf, d), lambda i, j, te, tv: (w2_base + te[i], wcol(i, j, te, tv), 0)),
    ]
    args = [xs, w1, w3, w2]
    out_specs = [pl.BlockSpec((tm, d), lambda i, j, te, tv: (i, 0))]
    out_shape = [jax.ShapeDtypeStruct((rows, d), F32)]
    if fuse_ln:
        resid, g, b = ln
        in_specs += [pl.BlockSpec((tm, d), lambda i, j, te, tv: (i, 0)),
                     pl.BlockSpec((1, d), lambda i, j, te, tv: (0, 0)),
                     pl.BlockSpec((1, d), lambda i, j, te, tv: (0, 0))]
        args += [resid, g.reshape(1, d), b.reshape(1, d)]
        out_specs.append(pl.BlockSpec((tm, d), lambda i, j, te, tv: (i, 0)))
        out_shape.append(jax.ShapeDtypeStruct((rows, d), BF16))
    out = pl.pallas_call(
        functools.partial(_ffn_kernel, fuse_ln=fuse_ln, sub=min(FFN_SUB_ROWS, tm)),
        grid_spec=pltpu.PrefetchScalarGridSpec(
            num_scalar_prefetch=2,
            grid=(rows // tm, nj),
            in_specs=in_specs,
            out_specs=out_specs,
            scratch_shapes=[pltpu.VMEM((tm, d), F32)] if fuse_ln else []),
        out_shape=out_shape,
        compiler_params=_params("parallel", "arbitrary"),
        name="ffn_ln" if fuse_ln else "ffn_experts",
    )(tile_expert, tile_valid, *args)
    return out if fuse_ln else out[0]


def _router_kernel(x_ref, w_ref, gate_ref, idx_ref):
    logits = jnp.dot(x_ref[...], w_ref[...], preferred_element_type=F32)
    col = lax.broadcasted_iota(jnp.int32, logits.shape, 1)
    neg = jnp.float32(-jnp.inf)
    logits = jnp.where(col < N_EXPERTS, logits, neg)
    m1 = jnp.max(logits, axis=-1, keepdims=True)
    i1 = jnp.min(jnp.where(logits == m1, col, LANE), axis=-1, keepdims=True)
    rest = jnp.where(col == i1, neg, logits)
    m2 = jnp.max(rest, axis=-1, keepdims=True)
    i2 = jnp.min(jnp.where(rest == m2, col, LANE), axis=-1, keepdims=True)
    e1 = jnp.exp(m1 - m1)
    e2 = jnp.exp(m2 - m1)
    total = e1 + e2
    gate_ref[...] = jnp.where(col == 0, e1 / total, jnp.where(col == 1, e2 / total, 0.0))
    idx_ref[...] = jnp.where(col == 0, i1, jnp.where(col == 1, i2, 0))


def _router(x16, w_pad, *, tm):
    m, d = x16.shape
    tm = min(tm, m)
    return pl.pallas_call(
        _router_kernel,
        grid=(m // tm,),
        in_specs=[pl.BlockSpec((tm, d), lambda i: (i, 0)),
                  pl.BlockSpec((d, LANE), lambda i: (0, 0))],
        out_specs=[pl.BlockSpec((tm, LANE), lambda i: (i, 0)),
                   pl.BlockSpec((tm, LANE), lambda i: (i, 0))],
        out_shape=[jax.ShapeDtypeStruct((m, LANE), F32),
                   jax.ShapeDtypeStruct((m, LANE), jnp.int32)],
        compiler_params=_params("parallel"),
        name="router",
    )(x16, w_pad)


def _combine_ln_kernel(r_ref, y0_ref, y1_ref, gate_ref, g_ref, b_ref, o_ref, o16_ref):
    gates = gate_ref[...]
    ff = gates[:, 0:1] * y0_ref[...] + gates[:, 1:2] * y1_ref[...]
    y = _layer_norm(DEEPNORM_ALPHA * r_ref[...] + ff, g_ref[...], b_ref[...])
    o_ref[...] = y
    o16_ref[...] = y.astype(BF16)


def _combine_ln(resid, y0, y1, gates, g, b, *, tm):
    m, d = resid.shape
    tm = min(tm, m)
    row = pl.BlockSpec((tm, d), lambda i: (i, 0))
    vec = pl.BlockSpec((1, d), lambda i: (0, 0))
    return pl.pallas_call(
        _combine_ln_kernel,
        grid=(m // tm,),
        in_specs=[row, row, row, pl.BlockSpec((tm, LANE), lambda i: (i, 0)), vec, vec],
        out_specs=[row, row],
        out_shape=[jax.ShapeDtypeStruct((m, d), F32), jax.ShapeDtypeStruct((m, d), BF16)],
        compiler_params=_params("parallel"),
        name="combine_ln",
    )(resid, y0, y1, gates, g.reshape(1, d), b.reshape(1, d))


FFN_TILE_F = 512
FFN_SUB_ROWS = 256
GLA_PROJ_TILES = 5
GLA_BLOCK = 1024


def _cast_cols_kernel(w_ref, o_ref):
    n = w_ref.shape[1]
    o_ref[:, :n] = w_ref[...].astype(BF16)
    if o_ref.shape[1] > n:
        o_ref[:, n:] = jnp.zeros((o_ref.shape[0], o_ref.shape[1] - n), BF16)


def _bf16_cols_padded(w, n_out, *, layer=None, tr=256):
    n = w.shape[-1]
    lead = w.shape[:-1] if layer is None else w.shape[1:-1]
    w2d = w.reshape(-1, n)
    rows = w2d.shape[0] if layer is None else w2d.shape[0] // w.shape[0]
    first = 0 if layer is None else layer * (rows // tr)
    out = pl.pallas_call(
        _cast_cols_kernel,
        grid=(rows // tr,),
        in_specs=[pl.BlockSpec((tr, n), lambda i: (first + i, 0))],
        out_specs=pl.BlockSpec((tr, n_out), lambda i: (i, 0)),
        out_shape=jax.ShapeDtypeStruct((rows, n_out), BF16),
        compiler_params=_params("parallel"),
        name="cast_cols",
    )(w2d)
    return out.reshape(*lead, n_out)


def _bf16_padded(w, axis, size):
    pad = [(0, 0)] * w.ndim
    pad[axis % w.ndim] = (0, size - w.shape[axis])
    return jnp.pad(w.astype(BF16), pad)


def _gla_proj_cols(w_in):
    main = w_in.shape[-1] - 2 * B_GATE_RANK
    return _round_up(main + LANE, GLA_PROJ_TILES * MXU_DIM)


def _gmlp_layer(x, x16, layer, w_in, v_ln_g, v_ln_b, w_s, b_s, w_out, ln_g, ln_b):
    z = _mm_act(x16, w_in, layer, gelu=True, out_dtype=BF16, tm=1024, tn=1024)
    h = _spatial_gate(z, v_ln_g, v_ln_b, w_s.astype(BF16), b_s.T, chunks=2)
    return _mm_ln(h, w_out, layer, x, ln_g, ln_b, tm=512, sub=128)


def _gla_layer(x, x16, bsz, layer, w_in, w_gk_f, bias_f, w_gk_b, bias_b, norm_g, w_out, ln_g, ln_b):
    t, d = x.shape
    seq = t // bsz
    key_dim = w_gk_f.shape[1]
    cols = w_in.shape[2]
    proj = _mm_act(x16, w_in, layer, gelu=False, out_dtype=F32, tm=1024, tn=cols // GLA_PROJ_TILES)
    proj = proj.reshape(bsz, seq, cols)
    wf = jnp.zeros((LANE, key_dim), BF16).at[:B_GATE_RANK].set(w_gk_f.astype(BF16))
    wb = jnp.zeros((LANE, key_dim), BF16).at[B_GATE_RANK:2 * B_GATE_RANK].set(w_gk_b.astype(BF16))
    o_fwd = _gla_direction(proj, wf, bias_f, reverse=False, blk=GLA_BLOCK)
    h = _gla_direction(proj, wb, bias_b, reverse=True, blk=GLA_BLOCK, o_fwd=o_fwd, norm_g=norm_g)
    return _mm_ln(h.reshape(t, -1), w_out, layer, x, ln_g, ln_b, tm=512, sub=128)


def _dense_ffn_layer(x, x16, layer, w1, w3, w2, ln_g, ln_b, *, tm):
    t = x.shape[0]
    tm = min(tm, t)
    n_tiles = t // tm
    return _ffn(x16, w1, w3, w2, jnp.full((n_tiles,), layer, jnp.int32), jnp.ones((n_tiles,), jnp.int32),
                tm=tm, tf=FFN_TILE_F, ln=(x, ln_g, ln_b))


def _moe_layer(x, x16, layer, router, w1, w3, w2, ln_g, ln_b, *, tm):
    t, d = x.shape
    tm = min(tm, t)
    router_p = jnp.pad(router.astype(BF16), ((0, 0), (0, LANE - N_EXPERTS)))
    gates, idx = _router(x16, router_p, tm=1024)

    pair_expert = idx[:, :TOP_K].reshape(-1)
    onehot = (pair_expert[:, None] == jnp.arange(N_EXPERTS, dtype=jnp.int32)[None, :]).astype(jnp.int32)
    rank = jnp.take_along_axis(jnp.cumsum(onehot, axis=0) - onehot, pair_expert[:, None], axis=1)[:, 0]
    counts = jnp.sum(onehot, axis=0)
    padded = (counts + tm - 1) // tm * tm
    ends = jnp.cumsum(padded)
    starts = ends - padded
    pair_row = starts[pair_expert] + rank
    n_tiles = (t * TOP_K) // tm + N_EXPERTS
    rows = n_tiles * tm
    row_token = jnp.zeros((rows,), jnp.int32).at[pair_row].set(
        jnp.arange(t * TOP_K, dtype=jnp.int32) // TOP_K)
    tile_start = jnp.arange(n_tiles, dtype=jnp.int32) * tm
    tile_valid = (tile_start < ends[-1]).astype(jnp.int32)
    tile_expert = jnp.minimum(jnp.sum((tile_start[:, None] >= ends[None, :]).astype(jnp.int32), axis=1),
                              N_EXPERTS - 1)
    last_expert = jnp.max(jnp.where(counts > 0, jnp.arange(N_EXPERTS, dtype=jnp.int32), 0))
    tile_expert = jnp.where(tile_valid != 0, tile_expert, last_expert)

    xs = x16.at[row_token].get(mode="promise_in_bounds")
    ys = _ffn(xs, w1, w3, w2, tile_expert, tile_valid, tm=tm, tf=FFN_TILE_F, w2_base=layer * N_EXPERTS)
    pos = pair_row.reshape(t, TOP_K)
    y0 = ys.at[pos[:, 0]].get(mode="promise_in_bounds")
    y1 = ys.at[pos[:, 1]].get(mode="promise_in_bounds")
    return _combine_ln(x, y0, y1, gates, ln_g, ln_b, tm=512)


def kernel(x, a_w_in, a_v_ln_g, a_v_ln_b, a_w_s, a_b_s, a_w_out, b_w_in, b_w_gk_f, b_gk_bias_f, b_w_gk_b, b_gk_bias_b, b_norm_g, b_w_out, ffn_w1, ffn_w3, ffn_w2, moe_router, moe_w1, moe_w3, moe_w2, ln_g, ln_b):
    bsz, seq, d = x.shape
    x = x.reshape(bsz * seq, d)
    x16 = x.astype(BF16)

    f_pad = _round_up(ffn_w1.shape[-1], FFN_TILE_F)
    a_w_in = _bf16_cols_padded(a_w_in, a_w_in.shape[-1])
    a_w_out = _bf16_cols_padded(a_w_out, a_w_out.shape[-1])
    b_w_in = _bf16_padded(b_w_in, -1, _gla_proj_cols(b_w_in))
    b_w_out = _bf16_cols_padded(b_w_out, b_w_out.shape[-1])
    ffn_w1 = _bf16_cols_padded(ffn_w1, f_pad)
    ffn_w3 = _bf16_cols_padded(ffn_w3, f_pad)
    ffn_w2 = _bf16_padded(ffn_w2, -2, f_pad)
    moe_w2 = _bf16_padded(moe_w2, -2, f_pad).reshape(-1, f_pad, d)

    for i in range(DEPTH):
        j = i // 2
        if i % 2 == 0:
            x, x16 = _gmlp_layer(x, x16, j, a_w_in, a_v_ln_g[j], a_v_ln_b[j], a_w_s[j], a_b_s[j],
                                 a_w_out, ln_g[i, 0], ln_b[i, 0])
            x, x16 = _dense_ffn_layer(x, x16, j, ffn_w1, ffn_w3, ffn_w2, ln_g[i, 1], ln_b[i, 1], tm=512)
        else:
            x, x16 = _gla_layer(x, x16, bsz, j, b_w_in, b_w_gk_f[j], b_gk_bias_f[j], b_w_gk_b[j],
                                b_gk_bias_b[j], b_norm_g[j], b_w_out, ln_g[i, 0], ln_b[i, 0])
            w1 = _bf16_cols_padded(moe_w1, f_pad, layer=j)
            w3 = _bf16_cols_padded(moe_w3, f_pad, layer=j)
            x, x16 = _moe_layer(x, x16, j, moe_router[j], w1, w3, moe_w2,
                                ln_g[i, 1], ln_b[i, 1], tm=512)
    return x.reshape(bsz, seq, d)
```

```python
import functools

import jax
import jax.numpy as jnp
from jax import lax
from jax.experimental import pallas as pl
from jax.experimental.pallas import tpu as pltpu

F32 = jnp.float32
BF16 = jnp.bfloat16

DEPTH = 4
A_CHUNK = 128
A_GROUPS = 8
B_HEADS = 4
B_GATE_RANK = 16
B_GATE_NORMALIZER = 16.0
B_CHUNK = 64
N_EXPERTS = 8
TOP_K = 2
DEEPNORM_ALPHA = (2.0 * DEPTH) ** 0.25
LN_EPS = 1e-5
RMS_EPS = 1e-5

LANE = 128
MXU_DIM = 256
VMEM_LIMIT_BYTES = 56 * 1024 * 1024


def _round_up(n, m):
    return (n + m - 1) // m * m


def _params(*semantics):
    return pltpu.CompilerParams(dimension_semantics=semantics,
                                vmem_limit_bytes=VMEM_LIMIT_BYTES)


def _layer_norm(y, g, b):
    mu = jnp.mean(y, axis=-1, keepdims=True)
    d = y - mu
    var = jnp.mean(d * d, axis=-1, keepdims=True)
    return d * lax.rsqrt(var + LN_EPS) * g + b


def _mm_act_kernel(x_ref, w_ref, o_ref, *, gelu):
    acc = jnp.dot(x_ref[...], w_ref[...], preferred_element_type=F32)
    if gelu:
        acc = 0.5 * acc * (1.0 + lax.erf(acc * (2.0 ** -0.5)))
    o_ref[...] = acc.astype(o_ref.dtype)


def _mm_act(x, w, layer, *, gelu, out_dtype, tm, tn):
    m, k = x.shape
    n = w.shape[2]
    tm = min(tm, m)
    return pl.pallas_call(
        functools.partial(_mm_act_kernel, gelu=gelu),
        grid=(m // tm, n // tn),
        in_specs=[pl.BlockSpec((tm, k), lambda i, j: (i, 0)),
                  pl.BlockSpec((None, k, tn), lambda i, j: (layer, 0, j))],
        out_specs=pl.BlockSpec((tm, tn), lambda i, j: (i, j)),
        out_shape=jax.ShapeDtypeStruct((m, n), out_dtype),
        compiler_params=_params("parallel", "arbitrary"),
        name="mm_act",
    )(x, w)


def _mm_ln_kernel(h_ref, w_ref, r_ref, g_ref, b_ref, o_ref, o16_ref, *, sub):
    for s in range(h_ref.shape[0] // sub):
        rows = slice(s * sub, (s + 1) * sub)
        acc = jnp.dot(h_ref[rows, :], w_ref[...], preferred_element_type=F32)
        y = _layer_norm(DEEPNORM_ALPHA * r_ref[rows, :] + acc, g_ref[...], b_ref[...])
        o_ref[rows, :] = y
        o16_ref[rows, :] = y.astype(BF16)


def _mm_ln(h, w, layer, resid, g, b, *, tm, sub):
    m, kdim = h.shape
    n = w.shape[2]
    tm = min(tm, m)
    return pl.pallas_call(
        functools.partial(_mm_ln_kernel, sub=sub),
        grid=(m // tm,),
        in_specs=[pl.BlockSpec((tm, kdim), lambda i: (i, 0)),
                  pl.BlockSpec((None, kdim, n), lambda i: (layer, 0, 0),
                               pipeline_mode=pl.Buffered(1)),
                  pl.BlockSpec((tm, n), lambda i: (i, 0)),
                  pl.BlockSpec((1, n), lambda i: (0, 0)),
                  pl.BlockSpec((1, n), lambda i: (0, 0))],
        out_specs=[pl.BlockSpec((tm, n), lambda i: (i, 0)),
                   pl.BlockSpec((tm, n), lambda i: (i, 0))],
        out_shape=[jax.ShapeDtypeStruct((m, n), F32),
                   jax.ShapeDtypeStruct((m, n), BF16)],
        compiler_params=_params("parallel"),
        name="mm_ln",
    )(h, w, resid, g.reshape(1, n), b.reshape(1, n))


def _spatial_gate_kernel(z_ref, g_ref, b_ref, ws_ref, bs_ref, o_ref, *, width, chunks):
    gdim = width // A_GROUPS
    v = z_ref[:, width:].astype(F32)
    vn = _layer_norm(v, g_ref[...], b_ref[...]).astype(BF16)
    for grp in range(A_GROUPS):
        w = ws_ref[grp]
        bias = bs_ref[:, grp:grp + 1]
        cols = slice(grp * gdim, (grp + 1) * gdim)
        for c in range(chunks):
            rows = slice(c * A_CHUNK, (c + 1) * A_CHUNK)
            s = jnp.dot(w, vn[rows, cols], preferred_element_type=F32) + bias
            u = z_ref[rows, cols].astype(F32)
            o_ref[rows, cols] = (u * s).astype(BF16)


def _spatial_gate(z, ln_g, ln_b, w_s, b_s_t, *, chunks):
    m = z.shape[0]
    width = z.shape[1] // 2
    tm = chunks * A_CHUNK
    return pl.pallas_call(
        functools.partial(_spatial_gate_kernel, width=width, chunks=chunks),
        grid=(m // tm,),
        in_specs=[pl.BlockSpec((tm, 2 * width), lambda i: (i, 0)),
                  pl.BlockSpec((1, width), lambda i: (0, 0)),
                  pl.BlockSpec((1, width), lambda i: (0, 0)),
                  pl.BlockSpec((A_GROUPS, A_CHUNK, A_CHUNK), lambda i: (0, 0, 0)),
                  pl.BlockSpec((A_CHUNK, A_GROUPS), lambda i: (0, 0))],
        out_specs=pl.BlockSpec((tm, width), lambda i: (i, 0)),
        out_shape=jax.ShapeDtypeStruct((m, width), BF16),
        compiler_params=_params("parallel"),
        name="spatial_gate",
    )(z, ln_g.reshape(1, width), ln_b.reshape(1, width), w_s, b_s_t)


def _gla_kernel(*refs, reverse, n_chunks):
    if reverse:
        (q_ref, k_ref, v_ref, low_ref, wgk_ref, bias_ref,
         ofwd_ref, gate_ref, ng_ref, o_ref, state_ref) = refs
    else:
        q_ref, k_ref, v_ref, low_ref, wgk_ref, bias_ref, o_ref, state_ref = refs
    dk = q_ref.shape[-1]
    c_len = B_CHUNK

    @pl.when(pl.program_id(2) == 0)
    def _():
        state_ref[...] = jnp.zeros_like(state_ref)

    row = lax.broadcasted_iota(jnp.int32, (c_len, c_len), 0)
    col = lax.broadcasted_iota(jnp.int32, (c_len, c_len), 1)
    if reverse:
        keep = col > row
        mid, last = c_len // 2, 0
    else:
        keep = col <= row
        mid, last = c_len // 2 - 1, c_len - 1

    pre = jnp.dot(low_ref[0].astype(BF16), wgk_ref[...], preferred_element_type=F32) + bias_ref[...]
    log_a_all = jax.nn.log_sigmoid(pre) / B_GATE_NORMALIZER

    n_rows = log_a_all.shape[0]
    pos = lax.broadcasted_iota(jnp.int32, log_a_all.shape, 0) % c_len
    b_all = log_a_all
    step = 1
    while step < c_len:
        if reverse:
            shifted = pltpu.roll(b_all, n_rows - step, axis=0)
            b_all = b_all + jnp.where(pos < c_len - step, shifted, 0.0)
        else:
            shifted = pltpu.roll(b_all, step, axis=0)
            b_all = b_all + jnp.where(pos >= step, shifted, 0.0)
        step *= 2

    state = state_ref[...]

    for ci in range(n_chunks):
        c = (n_chunks - 1 - ci) if reverse else ci
        rows = slice(c * c_len, (c + 1) * c_len)
        q = q_ref[0, rows, :] * (dk ** -0.5)
        k = k_ref[0, rows, :]
        v = v_ref[0, rows, :].astype(BF16)
        b = b_all[rows, :]
        b_mid = b[mid:mid + 1, :]
        b_last = b[last:last + 1, :]
        q_rel = (q * jnp.exp(b - b_mid)).astype(BF16)
        k_rel = (k * jnp.exp(b_mid - b)).astype(BF16)
        scores = lax.dot_general(q_rel, k_rel, (((1,), (1,)), ((), ())),
                                 preferred_element_type=F32)
        scores = jnp.where(keep, scores, 0.0).astype(BF16)
        o = jnp.dot(scores, v, preferred_element_type=F32)
        q_inter = (q * jnp.exp(b)).astype(BF16)
        k_upd = (k * jnp.exp(b_last - b)).astype(BF16)
        o = o + lax.dot_general(q_inter, state.astype(BF16), (((1,), (1,)), ((), ())),
                                preferred_element_type=F32)
        state = state * jnp.exp(b_last) + lax.dot_general(
            v, k_upd, (((0,), (0,)), ((), ())), preferred_element_type=F32)
        if reverse:
            o = o + ofwd_ref[0, rows, :]
            o = o * lax.rsqrt(jnp.mean(o * o, axis=-1, keepdims=True) + RMS_EPS) * ng_ref[...]
            o = o * jax.nn.silu(gate_ref[0, rows, :])
        o_ref[0, rows, :] = o.astype(o_ref.dtype)

    state_ref[...] = state


def _gla_direction(proj, wgk, bias, *, reverse, blk, o_fwd=None, norm_g=None):
    bsz, seq, _ = proj.shape
    dk = wgk.shape[1] // B_HEADS
    dv = 2 * dk
    key_dim = dk * B_HEADS
    val_dim = dv * B_HEADS
    blk = min(blk, seq)
    nb = seq // blk
    low_block = (2 * key_dim + 2 * val_dim) // LANE

    def sblk(n):
        return (nb - 1 - n) if reverse else n

    in_specs = [
        pl.BlockSpec((1, blk, dk), lambda b, h, n: (b, sblk(n), h)),
        pl.BlockSpec((1, blk, dk), lambda b, h, n: (b, sblk(n), key_dim // dk + h)),
        pl.BlockSpec((1, blk, dv), lambda b, h, n: (b, sblk(n), 2 * key_dim // dv + h)),
        pl.BlockSpec((1, blk, LANE), lambda b, h, n: (b, sblk(n), low_block)),
        pl.BlockSpec((LANE, dk), lambda b, h, n: (0, h)),
        pl.BlockSpec((1, dk), lambda b, h, n: (0, h)),
    ]
    args = [proj, proj, proj, proj, wgk, bias.reshape(1, key_dim)]
    if reverse:
        in_specs += [
            pl.BlockSpec((1, blk, dv), lambda b, h, n: (b, sblk(n), h)),
            pl.BlockSpec((1, blk, dv), lambda b, h, n: (b, sblk(n), (2 * key_dim + val_dim) // dv + h)),
            pl.BlockSpec((1, dv), lambda b, h, n: (0, 0)),
        ]
        args += [o_fwd, proj, norm_g.reshape(1, dv)]
    return pl.pallas_call(
        functools.partial(_gla_kernel, reverse=reverse, n_chunks=blk // B_CHUNK),
        grid=(bsz, B_HEADS, nb),
        in_specs=in_specs,
        out_specs=pl.BlockSpec((1, blk, dv), lambda b, h, n: (b, sblk(n), h)),
        out_shape=jax.ShapeDtypeStruct((bsz, seq, val_dim), BF16 if reverse else F32),
        scratch_shapes=[pltpu.VMEM((dv, dk), F32)],
        compiler_params=_params("parallel", "parallel", "arbitrary"),
        name="gla_bwd" if reverse else "gla_fwd",
    )(*args)


def _ffn_kernel(te_ref, tv_ref, *refs, fuse_ln, sub):
    if fuse_ln:
        x_ref, w1_ref, w3_ref, w2_ref, r_ref, g_ref, b_ref, o_ref, o16_ref, acc_ref = refs
    else:
        x_ref, w1_ref, w3_ref, w2_ref, o_ref = refs
        acc_ref = o_ref
    i = pl.program_id(0)
    j = pl.program_id(1)
    last = pl.num_programs(1) - 1
    valid = tv_ref[i] != 0

    @pl.when(j == 0)
    def _():
        acc_ref[...] = jnp.zeros_like(acc_ref)

    def step(matmul, finish):
        n_rows = x_ref.shape[0]
        size = sub if finish else n_rows
        for s in range(n_rows // size):
            rows = slice(s * size, (s + 1) * size)
            acc = acc_ref[rows, :]
            if matmul:
                x = x_ref[rows, :]
                a = jnp.dot(x, w1_ref[...], preferred_element_type=F32)
                c = jnp.dot(x, w3_ref[...], preferred_element_type=F32)
                h = (jax.nn.silu(a) * c).astype(BF16)
                acc = acc + jnp.dot(h, w2_ref[...], preferred_element_type=F32)
            if finish:
                y = _layer_norm(DEEPNORM_ALPHA * r_ref[rows, :] + acc, g_ref[...], b_ref[...])
                o_ref[rows, :] = y
                o16_ref[rows, :] = y.astype(BF16)
            else:
                acc_ref[rows, :] = acc

    if fuse_ln:
        pl.when(jnp.logical_and(valid, j < last))(functools.partial(step, True, False))
        pl.when(jnp.logical_and(valid, j == last))(functools.partial(step, True, True))
        pl.when(jnp.logical_and(jnp.logical_not(valid), j == last))(functools.partial(step, False, True))
    else:
        pl.when(valid)(functools.partial(step, True, False))


def _ffn(xs, w1, w3, w2, tile_expert, tile_valid, *, tm, tf, ln=None):
    rows, d = xs.shape
    f = w1.shape[2]
    nj = f // tf
    fuse_ln = ln is not None

    def wcol(i, j, te, tv):
        return jnp.where(tv[i] != 0, j, nj - 1)

    in_specs = [
        pl.BlockSpec((tm, d), lambda i, j, te, tv: (i, 0)),
        pl.BlockSpec((None, d, tf), lambda i, j, te, tv: (te[i], 0, wcol(i, j, te, tv))),
        pl.BlockSpec((None, d, tf), lambda i, j, te, tv: (te[i], 0, wcol(i, j, te, tv))),
        pl.BlockSpec((None, tf, d), lambda i, j, te, tv: (te[i], wcol(i, j, te, tv), 0)),
    ]
    args = [xs, w1, w3, w2]
    out_specs = [pl.BlockSpec((tm, d), lambda i, j, te, tv: (i, 0))]
    out_shape = [jax.ShapeDtypeStruct((rows, d), F32)]
    if fuse_ln:
        resid, g, b = ln
        in_specs += [pl.BlockSpec((tm, d), lambda i, j, te, tv: (i, 0)),
                     pl.BlockSpec((1, d), lambda i, j, te, tv: (0, 0)),
                     pl.BlockSpec((1, d), lambda i, j, te, tv: (0, 0))]
        args += [resid, g.reshape(1, d), b.reshape(1, d)]
        out_specs.append(pl.BlockSpec((tm, d), lambda i, j, te, tv: (i, 0)))
        out_shape.append(jax.ShapeDtypeStruct((rows, d), BF16))
    out = pl.pallas_call(
        functools.partial(_ffn_kernel, fuse_ln=fuse_ln, sub=min(FFN_SUB_ROWS, tm)),
        grid_spec=pltpu.PrefetchScalarGridSpec(
            num_scalar_prefetch=2,
            grid=(rows // tm, nj),
            in_specs=in_specs,
            out_specs=out_specs,
            scratch_shapes=[pltpu.VMEM((tm, d), F32)] if fuse_ln else []),
        out_shape=out_shape,
        compiler_params=_params("parallel", "arbitrary"),
        name="ffn_ln" if fuse_ln else "ffn_experts",
    )(tile_expert, tile_valid, *args)
    return out if fuse_ln else out[0]


def _router_kernel(x_ref, w_ref, gate_ref, idx_ref):
    logits = jnp.dot(x_ref[...], w_ref[...], preferred_element_type=F32)
    col = lax.broadcasted_iota(jnp.int32, logits.shape, 1)
    neg = jnp.float32(-jnp.inf)
    logits = jnp.where(col < N_EXPERTS, logits, neg)
    m1 = jnp.max(logits, axis=-1, keepdims=True)
    i1 = jnp.min(jnp.where(logits == m1, col, LANE), axis=-1, keepdims=True)
    rest = jnp.where(col == i1, neg, logits)
    m2 = jnp.max(rest, axis=-1, keepdims=True)
    i2 = jnp.min(jnp.where(rest == m2, col, LANE), axis=-1, keepdims=True)
    e1 = jnp.exp(m1 - m1)
    e2 = jnp.exp(m2 - m1)
    total = e1 + e2
    gate_ref[...] = jnp.where(col == 0, e1 / total, jnp.where(col == 1, e2 / total, 0.0))
    idx_ref[...] = jnp.where(col == 0, i1, jnp.where(col == 1, i2, 0))


def _router(x16, w_pad, *, tm):
    m, d = x16.shape
    tm = min(tm, m)
    return pl.pallas_call(
        _router_kernel,
        grid=(m // tm,),
        in_specs=[pl.BlockSpec((tm, d), lambda i: (i, 0)),
                  pl.BlockSpec((d, LANE), lambda i: (0, 0))],
        out_specs=[pl.BlockSpec((tm, LANE), lambda i: (i, 0)),
                   pl.BlockSpec((tm, LANE), lambda i: (i, 0))],
        out_shape=[jax.ShapeDtypeStruct((m, LANE), F32),
                   jax.ShapeDtypeStruct((m, LANE), jnp.int32)],
        compiler_params=_params("parallel"),
        name="router",
    )(x16, w_pad)


def _combine_ln_kernel(r_ref, y0_ref, y1_ref, gate_ref, g_ref, b_ref, o_ref, o16_ref):
    gates = gate_ref[...]
    ff = gates[:, 0:1] * y0_ref[...] + gates[:, 1:2] * y1_ref[...]
    y = _layer_norm(DEEPNORM_ALPHA * r_ref[...] + ff, g_ref[...], b_ref[...])
    o_ref[...] = y
    o16_ref[...] = y.astype(BF16)


def _combine_ln(resid, y0, y1, gates, g, b, *, tm):
    m, d = resid.shape
    tm = min(tm, m)
    row = pl.BlockSpec((tm, d), lambda i: (i, 0))
    vec = pl.BlockSpec((1, d), lambda i: (0, 0))
    return pl.pallas_call(
        _combine_ln_kernel,
        grid=(m // tm,),
        in_specs=[row, row, row, pl.BlockSpec((tm, LANE), lambda i: (i, 0)), vec, vec],
        out_specs=[row, row],
        out_shape=[jax.ShapeDtypeStruct((m, d), F32), jax.ShapeDtypeStruct((m, d), BF16)],
        compiler_params=_params("parallel"),
        name="combine_ln",
    )(resid, y0, y1, gates, g.reshape(1, d), b.reshape(1, d))


FFN_TILE_F = 512
FFN_SUB_ROWS = 256
MOE_TILE_ROWS = 1024
GLA_PROJ_TILES = 5
GLA_BLOCK = 1024


def _cast_cols_kernel(w_ref, o_ref):
    n = w_ref.shape[1]
    o_ref[:, :n] = w_ref[...].astype(BF16)
    if o_ref.shape[1] > n:
        o_ref[:, n:] = jnp.zeros((o_ref.shape[0], o_ref.shape[1] - n), BF16)


def _bf16_cols_padded(w, n_out, *, layer=None, tr=256):
    n = w.shape[-1]
    lead = w.shape[:-1] if layer is None else w.shape[1:-1]
    w2d = w.reshape(-1, n)
    rows = w2d.shape[0] if layer is None else w2d.shape[0] // w.shape[0]
    first = 0 if layer is None else layer * (rows // tr)
    out = pl.pallas_call(
        _cast_cols_kernel,
        grid=(rows // tr,),
        in_specs=[pl.BlockSpec((tr, n), lambda i: (first + i, 0))],
        out_specs=pl.BlockSpec((tr, n_out), lambda i: (i, 0)),
        out_shape=jax.ShapeDtypeStruct((rows, n_out), BF16),
        compiler_params=_params("parallel"),
        name="cast_cols",
    )(w2d)
    return out.reshape(*lead, n_out)


def _cast_rows_kernel(w_ref, o_ref, *, tail):
    r = pl.program_id(1)
    last = pl.num_programs(1) - 1

    @pl.when(r < last)
    def _():
        o_ref[...] = w_ref[...].astype(BF16)

    @pl.when(r == last)
    def _():
        o_ref[:tail, :] = w_ref[:tail, :].astype(BF16)
        if tail < o_ref.shape[0]:
            o_ref[tail:, :] = jnp.zeros((o_ref.shape[0] - tail, o_ref.shape[1]), BF16)


def _bf16_rows_padded(w, f_out, *, layer=None, tr=512):
    f, d = w.shape[-2:]
    w3d = w.reshape(-1, f, d)
    groups = w3d.shape[0] if layer is None else w3d.shape[0] // w.shape[0]
    first = 0 if layer is None else layer * groups
    n_blocks = f_out // tr
    tail = f - (n_blocks - 1) * tr
    assert f_out % tr == 0 and 0 < tail <= tr
    return pl.pallas_call(
        functools.partial(_cast_rows_kernel, tail=tail),
        grid=(groups, n_blocks),
        in_specs=[pl.BlockSpec((None, tr, d), lambda g, r: (first + g, r, 0))],
        out_specs=pl.BlockSpec((None, tr, d), lambda g, r: (g, r, 0)),
        out_shape=jax.ShapeDtypeStruct((groups, f_out, d), BF16),
        compiler_params=_params("parallel", "arbitrary"),
        name="cast_rows",
    )(w3d)


def _bf16_padded(w, axis, size):
    pad = [(0, 0)] * w.ndim
    pad[axis % w.ndim] = (0, size - w.shape[axis])
    return jnp.pad(w.astype(BF16), pad)


def _gla_proj_cols(w_in):
    main = w_in.shape[-1] - 2 * B_GATE_RANK
    return _round_up(main + LANE, GLA_PROJ_TILES * MXU_DIM)


def _gmlp_layer(x, x16, layer, w_in, v_ln_g, v_ln_b, w_s, b_s, w_out, ln_g, ln_b):
    z = _mm_act(x16, w_in, layer, gelu=True, out_dtype=BF16, tm=1024, tn=1024)
    h = _spatial_gate(z, v_ln_g, v_ln_b, w_s.astype(BF16), b_s.T, chunks=2)
    return _mm_ln(h, w_out, layer, x, ln_g, ln_b, tm=512, sub=128)


def _gla_layer(x, x16, bsz, layer, w_in, w_gk_f, bias_f, w_gk_b, bias_b, norm_g, w_out, ln_g, ln_b):
    t, d = x.shape
    seq = t // bsz
    key_dim = w_gk_f.shape[1]
    cols = w_in.shape[2]
    proj = _mm_act(x16, w_in, layer, gelu=False, out_dtype=F32, tm=1024, tn=cols // GLA_PROJ_TILES)
    proj = proj.reshape(bsz, seq, cols)
    wf = jnp.zeros((LANE, key_dim), BF16).at[:B_GATE_RANK].set(w_gk_f.astype(BF16))
    wb = jnp.zeros((LANE, key_dim), BF16).at[B_GATE_RANK:2 * B_GATE_RANK].set(w_gk_b.astype(BF16))
    o_fwd = _gla_direction(proj, wf, bias_f, reverse=False, blk=GLA_BLOCK)
    h = _gla_direction(proj, wb, bias_b, reverse=True, blk=GLA_BLOCK, o_fwd=o_fwd, norm_g=norm_g)
    return _mm_ln(h.reshape(t, -1), w_out, layer, x, ln_g, ln_b, tm=512, sub=128)


def _dense_ffn_layer(x, x16, layer, w1, w3, w2, ln_g, ln_b, *, tm):
    t = x.shape[0]
    tm = min(tm, t)
    n_tiles = t // tm
    return _ffn(x16, w1, w3, w2, jnp.full((n_tiles,), layer, jnp.int32), jnp.ones((n_tiles,), jnp.int32),
                tm=tm, tf=FFN_TILE_F, ln=(x, ln_g, ln_b))


def _moe_layer(x, x16, router, w1, w3, w2, ln_g, ln_b, *, tm):
    t, d = x.shape
    tm = min(tm, t)
    router_p = jnp.pad(router.astype(BF16), ((0, 0), (0, LANE - N_EXPERTS)))
    gates, idx = _router(x16, router_p, tm=1024)

    pair_expert = idx[:, :TOP_K].reshape(-1)
    onehot = (pair_expert[:, None] == jnp.arange(N_EXPERTS, dtype=jnp.int32)[None, :]).astype(jnp.int32)
    rank = jnp.take_along_axis(jnp.cumsum(onehot, axis=0) - onehot, pair_expert[:, None], axis=1)[:, 0]
    counts = jnp.sum(onehot, axis=0)
    padded = (counts + tm - 1) // tm * tm
    ends = jnp.cumsum(padded)
    starts = ends - padded
    pair_row = starts[pair_expert] + rank
    n_tiles = (t * TOP_K) // tm + N_EXPERTS
    rows = n_tiles * tm
    row_token = jnp.zeros((rows,), jnp.int32).at[pair_row].set(
        jnp.arange(t * TOP_K, dtype=jnp.int32) // TOP_K)
    tile_start = jnp.arange(n_tiles, dtype=jnp.int32) * tm
    tile_valid = (tile_start < ends[-1]).astype(jnp.int32)
    tile_expert = jnp.minimum(jnp.sum((tile_start[:, None] >= ends[None, :]).astype(jnp.int32), axis=1),
                              N_EXPERTS - 1)
    last_expert = jnp.max(jnp.where(counts > 0, jnp.arange(N_EXPERTS, dtype=jnp.int32), 0))
    tile_expert = jnp.where(tile_valid != 0, tile_expert, last_expert)

    xs = x16.at[row_token].get(mode="promise_in_bounds")
    ys = _ffn(xs, w1, w3, w2, tile_expert, tile_valid, tm=tm, tf=FFN_TILE_F)
    pos = pair_row.reshape(t, TOP_K)
    y0 = ys.at[pos[:, 0]].get(mode="promise_in_bounds")
    y1 = ys.at[pos[:, 1]].get(mode="promise_in_bounds")
    return _combine_ln(x, y0, y1, gates, ln_g, ln_b, tm=512)


def kernel(x, a_w_in, a_v_ln_g, a_v_ln_b, a_w_s, a_b_s, a_w_out, b_w_in, b_w_gk_f, b_gk_bias_f, b_w_gk_b, b_gk_bias_b, b_norm_g, b_w_out, ffn_w1, ffn_w3, ffn_w2, moe_router, moe_w1, moe_w3, moe_w2, ln_g, ln_b):
    bsz, seq, d = x.shape
    x = x.reshape(bsz * seq, d)
    x16 = x.astype(BF16)

    f_pad = _round_up(ffn_w1.shape[-1], FFN_TILE_F)
    a_w_in = _bf16_cols_padded(a_w_in, a_w_in.shape[-1])
    a_w_out = _bf16_cols_padded(a_w_out, a_w_out.shape[-1])
    b_w_in = _bf16_padded(b_w_in, -1, _gla_proj_cols(b_w_in))
    b_w_out = _bf16_cols_padded(b_w_out, b_w_out.shape[-1])
    ffn_w1 = _bf16_cols_padded(ffn_w1, f_pad)
    ffn_w3 = _bf16_cols_padded(ffn_w3, f_pad)
    ffn_w2 = _bf16_rows_padded(ffn_w2, f_pad)

    for i in range(DEPTH):
        j = i // 2
        if i % 2 == 0:
            x, x16 = _gmlp_layer(x, x16, j, a_w_in, a_v_ln_g[j], a_v_ln_b[j], a_w_s[j], a_b_s[j],
                                 a_w_out, ln_g[i, 0], ln_b[i, 0])
            x, x16 = _dense_ffn_layer(x, x16, j, ffn_w1, ffn_w3, ffn_w2, ln_g[i, 1], ln_b[i, 1], tm=512)
        else:
            x, x16 = _gla_layer(x, x16, bsz, j, b_w_in, b_w_gk_f[j], b_gk_bias_f[j], b_w_gk_b[j],
                                b_gk_bias_b[j], b_norm_g[j], b_w_out, ln_g[i, 0], ln_b[i, 0])
            w1 = _bf16_cols_padded(moe_w1, f_pad, layer=j)
            w3 = _bf16_cols_padded(moe_w3, f_pad, layer=j)
            w2 = _bf16_rows_padded(moe_w2, f_pad, layer=j)
            x, x16 = _moe_layer(x, x16, moe_router[j], w1, w3, w2,
                                ln_g[i, 1], ln_b[i, 1], tm=MOE_TILE_ROWS)
    return x.reshape(bsz, seq, d)
```

```python
import functools

import jax
import jax.numpy as jnp
from jax import lax
from jax.experimental import pallas as pl
from jax.experimental.pallas import tpu as pltpu

F32 = jnp.float32
BF16 = jnp.bfloat16

DEPTH = 4
A_CHUNK = 128
A_GROUPS = 8
B_HEADS = 4
B_GATE_RANK = 16
B_GATE_NORMALIZER = 16.0
B_CHUNK = 64
N_EXPERTS = 8
TOP_K = 2
DEEPNORM_ALPHA = (2.0 * DEPTH) ** 0.25
LN_EPS = 1e-5
RMS_EPS = 1e-5

LANE = 128
MXU_DIM = 256
VMEM_LIMIT_BYTES = 56 * 1024 * 1024


def _round_up(n, m):
    return (n + m - 1) // m * m


def _params(*semantics):
    return pltpu.CompilerParams(dimension_semantics=semantics,
                                vmem_limit_bytes=VMEM_LIMIT_BYTES)


def _layer_norm(y, g, b):
    mu = jnp.mean(y, axis=-1, keepdims=True)
    d = y - mu
    var = jnp.mean(d * d, axis=-1, keepdims=True)
    return d * lax.rsqrt(var + LN_EPS) * g + b


def _mm_act_kernel(x_ref, w_ref, o_ref, *, gelu):
    acc = jnp.dot(x_ref[...], w_ref[...], preferred_element_type=F32)
    if gelu:
        acc = 0.5 * acc * (1.0 + lax.erf(acc * (2.0 ** -0.5)))
    o_ref[...] = acc.astype(o_ref.dtype)


def _mm_act(x, w, layer, *, gelu, out_dtype, tm, tn):
    m, k = x.shape
    n = w.shape[2]
    tm = min(tm, m)
    return pl.pallas_call(
        functools.partial(_mm_act_kernel, gelu=gelu),
        grid=(m // tm, n // tn),
        in_specs=[pl.BlockSpec((tm, k), lambda i, j: (i, 0)),
                  pl.BlockSpec((None, k, tn), lambda i, j: (layer, 0, j))],
        out_specs=pl.BlockSpec((tm, tn), lambda i, j: (i, j)),
        out_shape=jax.ShapeDtypeStruct((m, n), out_dtype),
        compiler_params=_params("parallel", "arbitrary"),
        name="mm_act",
    )(x, w)


def _mm_ln_kernel(h_ref, w_ref, r_ref, g_ref, b_ref, o_ref, o16_ref, *, sub):
    for s in range(h_ref.shape[0] // sub):
        rows = slice(s * sub, (s + 1) * sub)
        acc = jnp.dot(h_ref[rows, :], w_ref[...], preferred_element_type=F32)
        y = _layer_norm(DEEPNORM_ALPHA * r_ref[rows, :] + acc, g_ref[...], b_ref[...])
        o_ref[rows, :] = y
        o16_ref[rows, :] = y.astype(BF16)


def _mm_ln(h, w, layer, resid, g, b, *, tm, sub):
    m, kdim = h.shape
    n = w.shape[2]
    tm = min(tm, m)
    return pl.pallas_call(
        functools.partial(_mm_ln_kernel, sub=sub),
        grid=(m // tm,),
        in_specs=[pl.BlockSpec((tm, kdim), lambda i: (i, 0)),
                  pl.BlockSpec((None, kdim, n), lambda i: (layer, 0, 0),
                               pipeline_mode=pl.Buffered(1)),
                  pl.BlockSpec((tm, n), lambda i: (i, 0)),
                  pl.BlockSpec((1, n), lambda i: (0, 0)),
                  pl.BlockSpec((1, n), lambda i: (0, 0))],
        out_specs=[pl.BlockSpec((tm, n), lambda i: (i, 0)),
                   pl.BlockSpec((tm, n), lambda i: (i, 0))],
        out_shape=[jax.ShapeDtypeStruct((m, n), F32),
                   jax.ShapeDtypeStruct((m, n), BF16)],
        compiler_params=_params("parallel"),
        name="mm_ln",
    )(h, w, resid, g.reshape(1, n), b.reshape(1, n))


def _spatial_gate_kernel(z_ref, g_ref, b_ref, ws_ref, bs_ref, o_ref, *, width, chunks):
    gdim = width // A_GROUPS
    v = z_ref[:, width:].astype(F32)
    vn = _layer_norm(v, g_ref[...], b_ref[...]).astype(BF16)
    for grp in range(A_GROUPS):
        w = ws_ref[grp]
        bias = bs_ref[:, grp:grp + 1]
        cols = slice(grp * gdim, (grp + 1) * gdim)
        for c in range(chunks):
            rows = slice(c * A_CHUNK, (c + 1) * A_CHUNK)
            s = jnp.dot(w, vn[rows, cols], preferred_element_type=F32) + bias
            u = z_ref[rows, cols].astype(F32)
            o_ref[rows, cols] = (u * s).astype(BF16)


def _spatial_gate(z, ln_g, ln_b, w_s, b_s_t, *, chunks):
    m = z.shape[0]
    width = z.shape[1] // 2
    tm = chunks * A_CHUNK
    return pl.pallas_call(
        functools.partial(_spatial_gate_kernel, width=width, chunks=chunks),
        grid=(m // tm,),
        in_specs=[pl.BlockSpec((tm, 2 * width), lambda i: (i, 0)),
                  pl.BlockSpec((1, width), lambda i: (0, 0)),
                  pl.BlockSpec((1, width), lambda i: (0, 0)),
                  pl.BlockSpec((A_GROUPS, A_CHUNK, A_CHUNK), lambda i: (0, 0, 0)),
                  pl.BlockSpec((A_CHUNK, A_GROUPS), lambda i: (0, 0))],
        out_specs=pl.BlockSpec((tm, width), lambda i: (i, 0)),
        out_shape=jax.ShapeDtypeStruct((m, width), BF16),
        compiler_params=_params("parallel"),
        name="spatial_gate",
    )(z, ln_g.reshape(1, width), ln_b.reshape(1, width), w_s, b_s_t)


def _gla_kernel(*refs, reverse, n_chunks):
    if reverse:
        (q_ref, k_ref, v_ref, low_ref, wgk_ref, bias_ref,
         ofwd_ref, gate_ref, ng_ref, o_ref, state_ref) = refs
    else:
        q_ref, k_ref, v_ref, low_ref, wgk_ref, bias_ref, o_ref, state_ref = refs
    dk = q_ref.shape[-1]
    c_len = B_CHUNK

    @pl.when(pl.program_id(2) == 0)
    def _():
        state_ref[...] = jnp.zeros_like(state_ref)

    row = lax.broadcasted_iota(jnp.int32, (c_len, c_len), 0)
    col = lax.broadcasted_iota(jnp.int32, (c_len, c_len), 1)
    if reverse:
        keep = col > row
        mid, last = c_len // 2, 0
    else:
        keep = col <= row
        mid, last = c_len // 2 - 1, c_len - 1

    pre = jnp.dot(low_ref[0].astype(BF16), wgk_ref[...], preferred_element_type=F32) + bias_ref[...]
    log_a_all = jax.nn.log_sigmoid(pre) / B_GATE_NORMALIZER

    n_rows = log_a_all.shape[0]
    pos = lax.broadcasted_iota(jnp.int32, log_a_all.shape, 0) % c_len
    b_all = log_a_all
    step = 1
    while step < c_len:
        if reverse:
            shifted = pltpu.roll(b_all, n_rows - step, axis=0)
            b_all = b_all + jnp.where(pos < c_len - step, shifted, 0.0)
        else:
            shifted = pltpu.roll(b_all, step, axis=0)
            b_all = b_all + jnp.where(pos >= step, shifted, 0.0)
        step *= 2

    state = state_ref[...]

    for ci in range(n_chunks):
        c = (n_chunks - 1 - ci) if reverse else ci
        rows = slice(c * c_len, (c + 1) * c_len)
        q = q_ref[0, rows, :] * (dk ** -0.5)
        k = k_ref[0, rows, :]
        v = v_ref[0, rows, :].astype(BF16)
        b = b_all[rows, :]
        b_mid = b[mid:mid + 1, :]
        b_last = b[last:last + 1, :]
        q_rel = (q * jnp.exp(b - b_mid)).astype(BF16)
        k_rel = (k * jnp.exp(b_mid - b)).astype(BF16)
        scores = lax.dot_general(q_rel, k_rel, (((1,), (1,)), ((), ())),
                                 preferred_element_type=F32)
        scores = jnp.where(keep, scores, 0.0).astype(BF16)
        o = jnp.dot(scores, v, preferred_element_type=F32)
        q_inter = (q * jnp.exp(b)).astype(BF16)
        k_upd = (k * jnp.exp(b_last - b)).astype(BF16)
        o = o + lax.dot_general(q_inter, state.astype(BF16), (((1,), (1,)), ((), ())),
                                preferred_element_type=F32)
        state = state * jnp.exp(b_last) + lax.dot_general(
            v, k_upd, (((0,), (0,)), ((), ())), preferred_element_type=F32)
        if reverse:
            o = o + ofwd_ref[0, rows, :]
            o = o * lax.rsqrt(jnp.mean(o * o, axis=-1, keepdims=True) + RMS_EPS) * ng_ref[...]
            o = o * jax.nn.silu(gate_ref[0, rows, :])
        o_ref[0, rows, :] = o.astype(o_ref.dtype)

    state_ref[...] = state


def _gla_direction(proj, wgk, bias, *, reverse, blk, o_fwd=None, norm_g=None):
    bsz, seq, _ = proj.shape
    dk = wgk.shape[1] // B_HEADS
    dv = 2 * dk
    key_dim = dk * B_HEADS
    val_dim = dv * B_HEADS
    blk = min(blk, seq)
    nb = seq // blk
    low_block = (2 * key_dim + 2 * val_dim) // LANE

    def sblk(n):
        return (nb - 1 - n) if reverse else n

    in_specs = [
        pl.BlockSpec((1, blk, dk), lambda b, h, n: (b, sblk(n), h)),
        pl.BlockSpec((1, blk, dk), lambda b, h, n: (b, sblk(n), key_dim // dk + h)),
        pl.BlockSpec((1, blk, dv), lambda b, h, n: (b, sblk(n), 2 * key_dim // dv + h)),
        pl.BlockSpec((1, blk, LANE), lambda b, h, n: (b, sblk(n), low_block)),
        pl.BlockSpec((LANE, dk), lambda b, h, n: (0, h)),
        pl.BlockSpec((1, dk), lambda b, h, n: (0, h)),
    ]
    args = [proj, proj, proj, proj, wgk, bias.reshape(1, key_dim)]
    if reverse:
        in_specs += [
            pl.BlockSpec((1, blk, dv), lambda b, h, n: (b, sblk(n), h)),
            pl.BlockSpec((1, blk, dv), lambda b, h, n: (b, sblk(n), (2 * key_dim + val_dim) // dv + h)),
            pl.BlockSpec((1, dv), lambda b, h, n: (0, 0)),
        ]
        args += [o_fwd, proj, norm_g.reshape(1, dv)]
    return pl.pallas_call(
        functools.partial(_gla_kernel, reverse=reverse, n_chunks=blk // B_CHUNK),
        grid=(bsz, B_HEADS, nb),
        in_specs=in_specs,
        out_specs=pl.BlockSpec((1, blk, dv), lambda b, h, n: (b, sblk(n), h)),
        out_shape=jax.ShapeDtypeStruct((bsz, seq, val_dim), BF16 if reverse else F32),
        scratch_shapes=[pltpu.VMEM((dv, dk), F32)],
        compiler_params=_params("parallel", "parallel", "arbitrary"),
        name="gla_bwd" if reverse else "gla_fwd",
    )(*args)


def _ffn_kernel(te_ref, tv_ref, *refs, fuse_ln, sub):
    if fuse_ln:
        x_ref, w1_ref, w3_ref, w2_ref, r_ref, g_ref, b_ref, o_ref, o16_ref, acc_ref = refs
    else:
        x_ref, w1_ref, w3_ref, w2_ref, o_ref, acc_ref = refs
    i = pl.program_id(0)
    j = pl.program_id(1)
    last = pl.num_programs(1) - 1
    valid = tv_ref[i] != 0

    @pl.when(j == 0)
    def _():
        acc_ref[...] = jnp.zeros_like(acc_ref)

    def step(matmul, finish):
        n_rows = x_ref.shape[0]
        size = sub if (finish and fuse_ln) else n_rows
        for s in range(n_rows // size):
            rows = slice(s * size, (s + 1) * size)
            acc = acc_ref[rows, :]
            if matmul:
                x = x_ref[rows, :]
                a = jnp.dot(x, w1_ref[...], preferred_element_type=F32)
                c = jnp.dot(x, w3_ref[...], preferred_element_type=F32)
                h = (jax.nn.silu(a) * c).astype(BF16)
                acc = acc + jnp.dot(h, w2_ref[...], preferred_element_type=F32)
            if not finish:
                acc_ref[rows, :] = acc
            elif fuse_ln:
                y = _layer_norm(DEEPNORM_ALPHA * r_ref[rows, :] + acc, g_ref[...], b_ref[...])
                o_ref[rows, :] = y
                o16_ref[rows, :] = y.astype(BF16)
            else:
                o_ref[rows, :] = acc.astype(o_ref.dtype)

    pl.when(jnp.logical_and(valid, j < last))(functools.partial(step, True, False))
    pl.when(jnp.logical_and(valid, j == last))(functools.partial(step, True, True))
    pl.when(jnp.logical_and(jnp.logical_not(valid), j == last))(functools.partial(step, False, True))


def _ffn(xs, w1, w3, w2, tile_expert, tile_valid, *, tm, tf, ln=None):
    rows, d = xs.shape
    f = w1.shape[2]
    nj = f // tf
    fuse_ln = ln is not None

    def wcol(i, j, te, tv):
        return jnp.where(tv[i] != 0, j, nj - 1)

    in_specs = [
        pl.BlockSpec((tm, d), lambda i, j, te, tv: (i, 0)),
        pl.BlockSpec((None, d, tf), lambda i, j, te, tv: (te[i], 0, wcol(i, j, te, tv))),
        pl.BlockSpec((None, d, tf), lambda i, j, te, tv: (te[i], 0, wcol(i, j, te, tv))),
        pl.BlockSpec((None, tf, d), lambda i, j, te, tv: (te[i], wcol(i, j, te, tv), 0)),
    ]
    args = [xs, w1, w3, w2]
    out_specs = [pl.BlockSpec((tm, d), lambda i, j, te, tv: (i, 0))]
    out_shape = [jax.ShapeDtypeStruct((rows, d), F32 if fuse_ln else BF16)]
    if fuse_ln:
        resid, g, b = ln
        in_specs += [pl.BlockSpec((tm, d), lambda i, j, te, tv: (i, 0)),
                     pl.BlockSpec((1, d), lambda i, j, te, tv: (0, 0)),
                     pl.BlockSpec((1, d), lambda i, j, te, tv: (0, 0))]
        args += [resid, g.reshape(1, d), b.reshape(1, d)]
        out_specs.append(pl.BlockSpec((tm, d), lambda i, j, te, tv: (i, 0)))
        out_shape.append(jax.ShapeDtypeStruct((rows, d), BF16))
    out = pl.pallas_call(
        functools.partial(_ffn_kernel, fuse_ln=fuse_ln, sub=min(FFN_SUB_ROWS, tm)),
        grid_spec=pltpu.PrefetchScalarGridSpec(
            num_scalar_prefetch=2,
            grid=(rows // tm, nj),
            in_specs=in_specs,
            out_specs=out_specs,
            scratch_shapes=[pltpu.VMEM((tm, d), F32)]),
        out_shape=out_shape,
        compiler_params=_params("parallel", "arbitrary"),
        name="ffn_ln" if fuse_ln else "ffn_experts",
    )(tile_expert, tile_valid, *args)
    return out if fuse_ln else out[0]


def _router_kernel(x_ref, w_ref, gate_ref, idx_ref):
    logits = jnp.dot(x_ref[...], w_ref[...], preferred_element_type=F32)
    col = lax.broadcasted_iota(jnp.int32, logits.shape, 1)
    neg = jnp.float32(-jnp.inf)
    logits = jnp.where(col < N_EXPERTS, logits, neg)
    m1 = jnp.max(logits, axis=-1, keepdims=True)
    i1 = jnp.min(jnp.where(logits == m1, col, LANE), axis=-1, keepdims=True)
    rest = jnp.where(col == i1, neg, logits)
    m2 = jnp.max(rest, axis=-1, keepdims=True)
    i2 = jnp.min(jnp.where(rest == m2, col, LANE), axis=-1, keepdims=True)
    e1 = jnp.exp(m1 - m1)
    e2 = jnp.exp(m2 - m1)
    total = e1 + e2
    gate_ref[...] = jnp.where(col == 0, e1 / total, jnp.where(col == 1, e2 / total, 0.0))
    idx_ref[...] = jnp.where(col == 0, i1, jnp.where(col == 1, i2, 0))


def _router(x16, w_pad, *, tm):
    m, d = x16.shape
    tm = min(tm, m)
    return pl.pallas_call(
        _router_kernel,
        grid=(m // tm,),
        in_specs=[pl.BlockSpec((tm, d), lambda i: (i, 0)),
                  pl.BlockSpec((d, LANE), lambda i: (0, 0))],
        out_specs=[pl.BlockSpec((tm, LANE), lambda i: (i, 0)),
                   pl.BlockSpec((tm, LANE), lambda i: (i, 0))],
        out_shape=[jax.ShapeDtypeStruct((m, LANE), F32),
                   jax.ShapeDtypeStruct((m, LANE), jnp.int32)],
        compiler_params=_params("parallel"),
        name="router",
    )(x16, w_pad)


def _combine_ln_kernel(r_ref, y0_ref, y1_ref, gate_ref, g_ref, b_ref, o_ref, o16_ref):
    gates = gate_ref[...]
    ff = gates[:, 0:1] * y0_ref[...].astype(F32) + gates[:, 1:2] * y1_ref[...].astype(F32)
    y = _layer_norm(DEEPNORM_ALPHA * r_ref[...] + ff, g_ref[...], b_ref[...])
    o_ref[...] = y
    o16_ref[...] = y.astype(BF16)


def _combine_ln(resid, y0, y1, gates, g, b, *, tm):
    m, d = resid.shape
    tm = min(tm, m)
    row = pl.BlockSpec((tm, d), lambda i: (i, 0))
    vec = pl.BlockSpec((1, d), lambda i: (0, 0))
    return pl.pallas_call(
        _combine_ln_kernel,
        grid=(m // tm,),
        in_specs=[row, row, row, pl.BlockSpec((tm, LANE), lambda i: (i, 0)), vec, vec],
        out_specs=[row, row],
        out_shape=[jax.ShapeDtypeStruct((m, d), F32), jax.ShapeDtypeStruct((m, d), BF16)],
        compiler_params=_params("parallel"),
        name="combine_ln",
    )(resid, y0, y1, gates, g.reshape(1, d), b.reshape(1, d))


FFN_TILE_F = 512
FFN_SUB_ROWS = 256
MOE_TILE_ROWS = 1024
GLA_PROJ_TILES = 5
GLA_BLOCK = 1024


def _cast_cols_kernel(w_ref, o_ref):
    n = w_ref.shape[1]
    o_ref[:, :n] = w_ref[...].astype(BF16)
    if o_ref.shape[1] > n:
        o_ref[:, n:] = jnp.zeros((o_ref.shape[0], o_ref.shape[1] - n), BF16)


def _bf16_cols_padded(w, n_out, *, tr=256):
    n = w.shape[-1]
    lead = w.shape[:-1]
    w2d = w.reshape(-1, n)
    rows = w2d.shape[0]
    out = pl.pallas_call(
        _cast_cols_kernel,
        grid=(rows // tr,),
        in_specs=[pl.BlockSpec((tr, n), lambda i: (i, 0))],
        out_specs=pl.BlockSpec((tr, n_out), lambda i: (i, 0)),
        out_shape=jax.ShapeDtypeStruct((rows, n_out), BF16),
        compiler_params=_params("parallel"),
        name="cast_cols",
    )(w2d)
    return out.reshape(*lead, n_out)


def _cast_rows_kernel(w_ref, o_ref, *, tail):
    r = pl.program_id(1)
    last = pl.num_programs(1) - 1

    @pl.when(r < last)
    def _():
        o_ref[...] = w_ref[...].astype(BF16)

    @pl.when(r == last)
    def _():
        o_ref[:tail, :] = w_ref[:tail, :].astype(BF16)
        if tail < o_ref.shape[0]:
            o_ref[tail:, :] = jnp.zeros((o_ref.shape[0] - tail, o_ref.shape[1]), BF16)


def _bf16_rows_padded(w, f_out, *, tr=512):
    f, d = w.shape[-2:]
    w3d = w.reshape(-1, f, d)
    groups = w3d.shape[0]
    n_blocks = f_out // tr
    tail = f - (n_blocks - 1) * tr
    assert f_out % tr == 0 and 0 < tail <= tr
    return pl.pallas_call(
        functools.partial(_cast_rows_kernel, tail=tail),
        grid=(groups, n_blocks),
        in_specs=[pl.BlockSpec((None, tr, d), lambda g, r: (g, r, 0))],
        out_specs=pl.BlockSpec((None, tr, d), lambda g, r: (g, r, 0)),
        out_shape=jax.ShapeDtypeStruct((groups, f_out, d), BF16),
        compiler_params=_params("parallel", "arbitrary"),
        name="cast_rows",
    )(w3d)


def _bf16_padded(w, axis, size):
    pad = [(0, 0)] * w.ndim
    pad[axis % w.ndim] = (0, size - w.shape[axis])
    return jnp.pad(w.astype(BF16), pad)


def _gla_proj_cols(w_in):
    main = w_in.shape[-1] - 2 * B_GATE_RANK
    return _round_up(main + LANE, GLA_PROJ_TILES * MXU_DIM)


def _gmlp_layer(x, x16, layer, w_in, v_ln_g, v_ln_b, w_s, b_s, w_out, ln_g, ln_b):
    z = _mm_act(x16, w_in, layer, gelu=True, out_dtype=BF16, tm=1024, tn=1024)
    h = _spatial_gate(z, v_ln_g, v_ln_b, w_s.astype(BF16), b_s.T, chunks=2)
    return _mm_ln(h, w_out, layer, x, ln_g, ln_b, tm=512, sub=128)


def _gla_layer(x, x16, bsz, layer, w_in, w_gk_f, bias_f, w_gk_b, bias_b, norm_g, w_out, ln_g, ln_b):
    t, d = x.shape
    seq = t // bsz
    key_dim = w_gk_f.shape[1]
    cols = w_in.shape[2]
    proj = _mm_act(x16, w_in, layer, gelu=False, out_dtype=F32, tm=1024, tn=cols // GLA_PROJ_TILES)
    proj = proj.reshape(bsz, seq, cols)
    wf = jnp.zeros((LANE, key_dim), BF16).at[:B_GATE_RANK].set(w_gk_f.astype(BF16))
    wb = jnp.zeros((LANE, key_dim), BF16).at[B_GATE_RANK:2 * B_GATE_RANK].set(w_gk_b.astype(BF16))
    o_fwd = _gla_direction(proj, wf, bias_f, reverse=False, blk=GLA_BLOCK)
    h = _gla_direction(proj, wb, bias_b, reverse=True, blk=GLA_BLOCK, o_fwd=o_fwd, norm_g=norm_g)
    return _mm_ln(h.reshape(t, -1), w_out, layer, x, ln_g, ln_b, tm=512, sub=128)


def _dense_ffn_layer(x, x16, layer, w1, w3, w2, ln_g, ln_b, *, tm):
    t = x.shape[0]
    tm = min(tm, t)
    n_tiles = t // tm
    return _ffn(x16, w1, w3, w2, jnp.full((n_tiles,), layer, jnp.int32), jnp.ones((n_tiles,), jnp.int32),
                tm=tm, tf=FFN_TILE_F, ln=(x, ln_g, ln_b))


def _moe_layer(x, x16, layer, router, w1, w3, w2, ln_g, ln_b, *, tm):
    t, d = x.shape
    tm = min(tm, t)
    router_p = jnp.pad(router.astype(BF16), ((0, 0), (0, LANE - N_EXPERTS)))
    gates, idx = _router(x16, router_p, tm=1024)

    pair_expert = idx[:, :TOP_K].reshape(-1)
    onehot = (pair_expert[:, None] == jnp.arange(N_EXPERTS, dtype=jnp.int32)[None, :]).astype(jnp.int32)
    rank = jnp.take_along_axis(jnp.cumsum(onehot, axis=0) - onehot, pair_expert[:, None], axis=1)[:, 0]
    counts = jnp.sum(onehot, axis=0)
    padded = (counts + tm - 1) // tm * tm
    ends = jnp.cumsum(padded)
    starts = ends - padded
    pair_row = starts[pair_expert] + rank
    n_tiles = (t * TOP_K) // tm + N_EXPERTS
    rows = n_tiles * tm
    row_token = jnp.zeros((rows,), jnp.int32).at[pair_row].set(
        jnp.arange(t * TOP_K, dtype=jnp.int32) // TOP_K)
    tile_start = jnp.arange(n_tiles, dtype=jnp.int32) * tm
    tile_valid = (tile_start < ends[-1]).astype(jnp.int32)
    tile_expert = jnp.minimum(jnp.sum((tile_start[:, None] >= ends[None, :]).astype(jnp.int32), axis=1),
                              N_EXPERTS - 1)
    last_expert = jnp.max(jnp.where(counts > 0, jnp.arange(N_EXPERTS, dtype=jnp.int32), 0))
    tile_expert = jnp.where(tile_valid != 0, tile_expert, last_expert)

    xs = x16.at[row_token].get(mode="promise_in_bounds")
    ys = _ffn(xs, w1, w3, w2, tile_expert + layer * N_EXPERTS, tile_valid, tm=tm, tf=FFN_TILE_F)
    pos = pair_row.reshape(t, TOP_K)
    y0 = ys.at[pos[:, 0]].get(mode="promise_in_bounds")
    y1 = ys.at[pos[:, 1]].get(mode="promise_in_bounds")
    return _combine_ln(x, y0, y1, gates, ln_g, ln_b, tm=512)


def kernel(x, a_w_in, a_v_ln_g, a_v_ln_b, a_w_s, a_b_s, a_w_out, b_w_in, b_w_gk_f, b_gk_bias_f, b_w_gk_b, b_gk_bias_b, b_norm_g, b_w_out, ffn_w1, ffn_w3, ffn_w2, moe_router, moe_w1, moe_w3, moe_w2, ln_g, ln_b):
    bsz, seq, d = x.shape
    x = x.reshape(bsz * seq, d)
    x16 = x.astype(BF16)

    f_pad = _round_up(ffn_w1.shape[-1], FFN_TILE_F)
    a_w_in = _bf16_cols_padded(a_w_in, a_w_in.shape[-1])
    a_w_out = _bf16_cols_padded(a_w_out, a_w_out.shape[-1])
    b_w_in = _bf16_padded(b_w_in, -1, _gla_proj_cols(b_w_in))
    b_w_out = _bf16_cols_padded(b_w_out, b_w_out.shape[-1])
    ffn_w1 = _bf16_cols_padded(ffn_w1, f_pad)
    ffn_w3 = _bf16_cols_padded(ffn_w3, f_pad)
    ffn_w2 = _bf16_rows_padded(ffn_w2, f_pad)
    moe_w1 = _bf16_cols_padded(moe_w1, f_pad).reshape(-1, d, f_pad)
    moe_w3 = _bf16_cols_padded(moe_w3, f_pad).reshape(-1, d, f_pad)
    moe_w2 = _bf16_rows_padded(moe_w2, f_pad)

    for i in range(DEPTH):
        j = i // 2
        if i % 2 == 0:
            x, x16 = _gmlp_layer(x, x16, j, a_w_in, a_v_ln_g[j], a_v_ln_b[j], a_w_s[j], a_b_s[j],
                                 a_w_out, ln_g[i, 0], ln_b[i, 0])
            x, x16 = _dense_ffn_layer(x, x16, j, ffn_w1, ffn_w3, ffn_w2, ln_g[i, 1], ln_b[i, 1], tm=512)
        else:
            x, x16 = _gla_layer(x, x16, bsz, j, b_w_in, b_w_gk_f[j], b_gk_bias_f[j], b_w_gk_b[j],
                                b_gk_bias_b[j], b_norm_g[j], b_w_out, ln_g[i, 0], ln_b[i, 0])
            x, x16 = _moe_layer(x, x16, j, moe_router[j], moe_w1, moe_w3, moe_w2,
                                ln_g[i, 1], ln_b[i, 1], tm=MOE_TILE_ROWS)
    return x.reshape(bsz, seq, d)
```

```python
import functools

import jax
import jax.numpy as jnp
from jax import lax
from jax.experimental import pallas as pl
from jax.experimental.pallas import tpu as pltpu

F32 = jnp.float32
BF16 = jnp.bfloat16

DEPTH = 4
A_CHUNK = 128
A_GROUPS = 8
B_HEADS = 4
B_GATE_RANK = 16
B_GATE_NORMALIZER = 16.0
B_CHUNK = 64
N_EXPERTS = 8
TOP_K = 2
DEEPNORM_ALPHA = (2.0 * DEPTH) ** 0.25
LN_EPS = 1e-5
RMS_EPS = 1e-5

LANE = 128
MXU_DIM = 256
VMEM_LIMIT_BYTES = 56 * 1024 * 1024


def _round_up(n, m):
    return (n + m - 1) // m * m


def _params(*semantics):
    return pltpu.CompilerParams(dimension_semantics=semantics,
                                vmem_limit_bytes=VMEM_LIMIT_BYTES)


def _layer_norm(y, g, b):
    mu = jnp.mean(y, axis=-1, keepdims=True)
    d = y - mu
    var = jnp.mean(d * d, axis=-1, keepdims=True)
    return d * lax.rsqrt(var + LN_EPS) * g + b


def _mm_act_kernel(x_ref, w_ref, o_ref, *, gelu):
    acc = jnp.dot(x_ref[...], w_ref[...], preferred_element_type=F32)
    if gelu:
        acc = 0.5 * acc * (1.0 + lax.erf(acc * (2.0 ** -0.5)))
    o_ref[...] = acc.astype(o_ref.dtype)


def _mm_act(x, w, layer, *, gelu, out_dtype, tm, tn):
    m, k = x.shape
    n = w.shape[2]
    tm = min(tm, m)
    return pl.pallas_call(
        functools.partial(_mm_act_kernel, gelu=gelu),
        grid=(m // tm, n // tn),
        in_specs=[pl.BlockSpec((tm, k), lambda i, j: (i, 0)),
                  pl.BlockSpec((None, k, tn), lambda i, j: (layer, 0, j))],
        out_specs=pl.BlockSpec((tm, tn), lambda i, j: (i, j)),
        out_shape=jax.ShapeDtypeStruct((m, n), out_dtype),
        compiler_params=_params("parallel", "arbitrary"),
        name="mm_act",
    )(x, w)


def _mm_ln_kernel(h_ref, w_ref, r_ref, g_ref, b_ref, o_ref, o16_ref, *, sub):
    for s in range(h_ref.shape[0] // sub):
        rows = slice(s * sub, (s + 1) * sub)
        acc = jnp.dot(h_ref[rows, :], w_ref[...], preferred_element_type=F32)
        y = _layer_norm(DEEPNORM_ALPHA * r_ref[rows, :] + acc, g_ref[...], b_ref[...])
        o_ref[rows, :] = y
        o16_ref[rows, :] = y.astype(BF16)


def _mm_ln(h, w, layer, resid, g, b, *, tm, sub):
    m, kdim = h.shape
    n = w.shape[2]
    tm = min(tm, m)
    return pl.pallas_call(
        functools.partial(_mm_ln_kernel, sub=sub),
        grid=(m // tm,),
        in_specs=[pl.BlockSpec((tm, kdim), lambda i: (i, 0)),
                  pl.BlockSpec((None, kdim, n), lambda i: (layer, 0, 0),
                               pipeline_mode=pl.Buffered(1)),
                  pl.BlockSpec((tm, n), lambda i: (i, 0)),
                  pl.BlockSpec((1, n), lambda i: (0, 0)),
                  pl.BlockSpec((1, n), lambda i: (0, 0))],
        out_specs=[pl.BlockSpec((tm, n), lambda i: (i, 0)),
                   pl.BlockSpec((tm, n), lambda i: (i, 0))],
        out_shape=[jax.ShapeDtypeStruct((m, n), F32),
                   jax.ShapeDtypeStruct((m, n), BF16)],
        compiler_params=_params("parallel"),
        name="mm_ln",
    )(h, w, resid, g.reshape(1, n), b.reshape(1, n))


def _spatial_gate_kernel(z_ref, g_ref, b_ref, ws_ref, bs_ref, o_ref, *, width, chunks):
    gdim = width // A_GROUPS
    v = z_ref[:, width:].astype(F32)
    vn = _layer_norm(v, g_ref[...], b_ref[...]).astype(BF16)
    for grp in range(A_GROUPS):
        w = ws_ref[grp]
        bias = bs_ref[:, grp:grp + 1]
        cols = slice(grp * gdim, (grp + 1) * gdim)
        for c in range(chunks):
            rows = slice(c * A_CHUNK, (c + 1) * A_CHUNK)
            s = jnp.dot(w, vn[rows, cols], preferred_element_type=F32) + bias
            u = z_ref[rows, cols].astype(F32)
            o_ref[rows, cols] = (u * s).astype(BF16)


def _spatial_gate(z, ln_g, ln_b, w_s, b_s_t, *, chunks):
    m = z.shape[0]
    width = z.shape[1] // 2
    tm = chunks * A_CHUNK
    return pl.pallas_call(
        functools.partial(_spatial_gate_kernel, width=width, chunks=chunks),
        grid=(m // tm,),
        in_specs=[pl.BlockSpec((tm, 2 * width), lambda i: (i, 0)),
                  pl.BlockSpec((1, width), lambda i: (0, 0)),
                  pl.BlockSpec((1, width), lambda i: (0, 0)),
                  pl.BlockSpec((A_GROUPS, A_CHUNK, A_CHUNK), lambda i: (0, 0, 0)),
                  pl.BlockSpec((A_CHUNK, A_GROUPS), lambda i: (0, 0))],
        out_specs=pl.BlockSpec((tm, width), lambda i: (i, 0)),
        out_shape=jax.ShapeDtypeStruct((m, width), BF16),
        compiler_params=_params("parallel"),
        name="spatial_gate",
    )(z, ln_g.reshape(1, width), ln_b.reshape(1, width), w_s, b_s_t)


def _gla_kernel(*refs, reverse, n_chunks):
    if reverse:
        (q_ref, k_ref, v_ref, low_ref, wgk_ref, bias_ref,
         ofwd_ref, gate_ref, ng_ref, o_ref, state_ref) = refs
    else:
        q_ref, k_ref, v_ref, low_ref, wgk_ref, bias_ref, o_ref, state_ref = refs
    dk = q_ref.shape[-1]
    c_len = B_CHUNK

    @pl.when(pl.program_id(2) == 0)
    def _():
        state_ref[...] = jnp.zeros_like(state_ref)

    row = lax.broadcasted_iota(jnp.int32, (c_len, c_len), 0)
    col = lax.broadcasted_iota(jnp.int32, (c_len, c_len), 1)
    if reverse:
        keep = col > row
        mid, last = c_len // 2, 0
    else:
        keep = col <= row
        mid, last = c_len // 2 - 1, c_len - 1

    pre = jnp.dot(low_ref[0].astype(BF16), wgk_ref[...], preferred_element_type=F32) + bias_ref[...]
    log_a_all = jax.nn.log_sigmoid(pre) / B_GATE_NORMALIZER

    n_rows = log_a_all.shape[0]
    pos = lax.broadcasted_iota(jnp.int32, log_a_all.shape, 0) % c_len
    b_all = log_a_all
    step = 1
    while step < c_len:
        if reverse:
            shifted = pltpu.roll(b_all, n_rows - step, axis=0)
            b_all = b_all + jnp.where(pos < c_len - step, shifted, 0.0)
        else:
            shifted = pltpu.roll(b_all, step, axis=0)
            b_all = b_all + jnp.where(pos >= step, shifted, 0.0)
        step *= 2

    state = state_ref[...]

    for ci in range(n_chunks):
        c = (n_chunks - 1 - ci) if reverse else ci
        rows = slice(c * c_len, (c + 1) * c_len)
        q = q_ref[0, rows, :] * (dk ** -0.5)
        k = k_ref[0, rows, :]
        v = v_ref[0, rows, :].astype(BF16)
        b = b_all[rows, :]
        b_mid = b[mid:mid + 1, :]
        b_last = b[last:last + 1, :]
        q_rel = (q * jnp.exp(b - b_mid)).astype(BF16)
        k_rel = (k * jnp.exp(b_mid - b)).astype(BF16)
        scores = lax.dot_general(q_rel, k_rel, (((1,), (1,)), ((), ())),
                                 preferred_element_type=F32)
        scores = jnp.where(keep, scores, 0.0).astype(BF16)
        o = jnp.dot(scores, v, preferred_element_type=F32)
        q_inter = (q * jnp.exp(b)).astype(BF16)
        k_upd = (k * jnp.exp(b_last - b)).astype(BF16)
        o = o + lax.dot_general(q_inter, state.astype(BF16), (((1,), (1,)), ((), ())),
                                preferred_element_type=F32)
        state = state * jnp.exp(b_last) + lax.dot_general(
            v, k_upd, (((0,), (0,)), ((), ())), preferred_element_type=F32)
        if reverse:
            o = o + ofwd_ref[0, rows, :]
            o = o * lax.rsqrt(jnp.mean(o * o, axis=-1, keepdims=True) + RMS_EPS) * ng_ref[...]
            o = o * jax.nn.silu(gate_ref[0, rows, :])
        o_ref[0, rows, :] = o.astype(o_ref.dtype)

    state_ref[...] = state


def _gla_direction(proj, wgk, bias, *, reverse, blk, o_fwd=None, norm_g=None):
    bsz, seq, _ = proj.shape
    dk = wgk.shape[1] // B_HEADS
    dv = 2 * dk
    key_dim = dk * B_HEADS
    val_dim = dv * B_HEADS
    blk = min(blk, seq)
    nb = seq // blk
    low_block = (2 * key_dim + 2 * val_dim) // LANE

    def sblk(n):
        return (nb - 1 - n) if reverse else n

    in_specs = [
        pl.BlockSpec((1, blk, dk), lambda b, h, n: (b, sblk(n), h)),
        pl.BlockSpec((1, blk, dk), lambda b, h, n: (b, sblk(n), key_dim // dk + h)),
        pl.BlockSpec((1, blk, dv), lambda b, h, n: (b, sblk(n), 2 * key_dim // dv + h)),
        pl.BlockSpec((1, blk, LANE), lambda b, h, n: (b, sblk(n), low_block)),
        pl.BlockSpec((LANE, dk), lambda b, h, n: (0, h)),
        pl.BlockSpec((1, dk), lambda b, h, n: (0, h)),
    ]
    args = [proj, proj, proj, proj, wgk, bias.reshape(1, key_dim)]
    if reverse:
        in_specs += [
            pl.BlockSpec((1, blk, dv), lambda b, h, n: (b, sblk(n), h)),
            pl.BlockSpec((1, blk, dv), lambda b, h, n: (b, sblk(n), (2 * key_dim + val_dim) // dv + h)),
            pl.BlockSpec((1, dv), lambda b, h, n: (0, 0)),
        ]
        args += [o_fwd, proj, norm_g.reshape(1, dv)]
    return pl.pallas_call(
        functools.partial(_gla_kernel, reverse=reverse, n_chunks=blk // B_CHUNK),
        grid=(bsz, B_HEADS, nb),
        in_specs=in_specs,
        out_specs=pl.BlockSpec((1, blk, dv), lambda b, h, n: (b, sblk(n), h)),
        out_shape=jax.ShapeDtypeStruct((bsz, seq, val_dim), BF16 if reverse else F32),
        scratch_shapes=[pltpu.VMEM((dv, dk), F32)],
        compiler_params=_params("parallel", "parallel", "arbitrary"),
        name="gla_bwd" if reverse else "gla_fwd",
    )(*args)


def _ffn_kernel(te_ref, tv_ref, *refs, fuse_ln, sub):
    if fuse_ln:
        x_ref, w1_ref, w3_ref, w2_ref, r_ref, g_ref, b_ref, o_ref, o16_ref, acc_ref = refs
    else:
        x_ref, w1_ref, w3_ref, w2_ref, o_ref, acc_ref = refs
    i = pl.program_id(0)
    j = pl.program_id(1)
    last = pl.num_programs(1) - 1
    valid = tv_ref[i] != 0

    @pl.when(j == 0)
    def _():
        acc_ref[...] = jnp.zeros_like(acc_ref)

    def step(matmul, finish):
        n_rows = x_ref.shape[0]
        size = sub if (finish and fuse_ln) else n_rows
        for s in range(n_rows // size):
            rows = slice(s * size, (s + 1) * size)
            acc = acc_ref[rows, :]
            if matmul:
                x = x_ref[rows, :]
                a = jnp.dot(x, w1_ref[...], preferred_element_type=F32)
                c = jnp.dot(x, w3_ref[...], preferred_element_type=F32)
                h = (jax.nn.silu(a) * c).astype(BF16)
                acc = acc + jnp.dot(h, w2_ref[...], preferred_element_type=F32)
            if not finish:
                acc_ref[rows, :] = acc
            elif fuse_ln:
                y = _layer_norm(DEEPNORM_ALPHA * r_ref[rows, :] + acc, g_ref[...], b_ref[...])
                o_ref[rows, :] = y
                o16_ref[rows, :] = y.astype(BF16)
            else:
                o_ref[rows, :] = acc.astype(o_ref.dtype)

    pl.when(jnp.logical_and(valid, j < last))(functools.partial(step, True, False))
    pl.when(jnp.logical_and(valid, j == last))(functools.partial(step, True, True))
    pl.when(jnp.logical_and(jnp.logical_not(valid), j == last))(functools.partial(step, False, True))


def _ffn(xs, w1, w3, w2, tile_expert, tile_valid, *, tm, tf, ln=None):
    rows, d = xs.shape
    f = w1.shape[2]
    nj = f // tf
    fuse_ln = ln is not None

    def wcol(i, j, te, tv):
        return jnp.where(tv[i] != 0, j, nj - 1)

    in_specs = [
        pl.BlockSpec((tm, d), lambda i, j, te, tv: (i, 0)),
        pl.BlockSpec((None, d, tf), lambda i, j, te, tv: (te[i], 0, wcol(i, j, te, tv))),
        pl.BlockSpec((None, d, tf), lambda i, j, te, tv: (te[i], 0, wcol(i, j, te, tv))),
        pl.BlockSpec((None, tf, d), lambda i, j, te, tv: (te[i], wcol(i, j, te, tv), 0)),
    ]
    args = [xs, w1, w3, w2]
    out_specs = [pl.BlockSpec((tm, d), lambda i, j, te, tv: (i, 0))]
    out_shape = [jax.ShapeDtypeStruct((rows, d), F32 if fuse_ln else BF16)]
    if fuse_ln:
        resid, g, b = ln
        in_specs += [pl.BlockSpec((tm, d), lambda i, j, te, tv: (i, 0)),
                     pl.BlockSpec((1, d), lambda i, j, te, tv: (0, 0)),
                     pl.BlockSpec((1, d), lambda i, j, te, tv: (0, 0))]
        args += [resid, g.reshape(1, d), b.reshape(1, d)]
        out_specs.append(pl.BlockSpec((tm, d), lambda i, j, te, tv: (i, 0)))
        out_shape.append(jax.ShapeDtypeStruct((rows, d), BF16))
    out = pl.pallas_call(
        functools.partial(_ffn_kernel, fuse_ln=fuse_ln, sub=min(FFN_SUB_ROWS, tm)),
        grid_spec=pltpu.PrefetchScalarGridSpec(
            num_scalar_prefetch=2,
            grid=(rows // tm, nj),
            in_specs=in_specs,
            out_specs=out_specs,
            scratch_shapes=[pltpu.VMEM((tm, d), F32)]),
        out_shape=out_shape,
        compiler_params=_params("parallel", "arbitrary"),
        name="ffn_ln" if fuse_ln else "ffn_experts",
    )(tile_expert, tile_valid, *args)
    return out if fuse_ln else out[0]


def _router_kernel(x_ref, w_ref, gate_ref, idx_ref):
    logits = jnp.dot(x_ref[...], w_ref[...], preferred_element_type=F32)
    col = lax.broadcasted_iota(jnp.int32, logits.shape, 1)
    neg = jnp.float32(-jnp.inf)
    logits = jnp.where(col < N_EXPERTS, logits, neg)
    m1 = jnp.max(logits, axis=-1, keepdims=True)
    i1 = jnp.min(jnp.where(logits == m1, col, LANE), axis=-1, keepdims=True)
    rest = jnp.where(col == i1, neg, logits)
    m2 = jnp.max(rest, axis=-1, keepdims=True)
    i2 = jnp.min(jnp.where(rest == m2, col, LANE), axis=-1, keepdims=True)
    e1 = jnp.exp(m1 - m1)
    e2 = jnp.exp(m2 - m1)
    total = e1 + e2
    gate_ref[...] = jnp.where(col == 0, e1 / total, jnp.where(col == 1, e2 / total, 0.0))
    idx_ref[...] = jnp.where(col == 0, i1, jnp.where(col == 1, i2, 0))


def _router(x16, w_pad, *, tm):
    m, d = x16.shape
    tm = min(tm, m)
    return pl.pallas_call(
        _router_kernel,
        grid=(m // tm,),
        in_specs=[pl.BlockSpec((tm, d), lambda i: (i, 0)),
                  pl.BlockSpec((d, LANE), lambda i: (0, 0))],
        out_specs=[pl.BlockSpec((tm, LANE), lambda i: (i, 0)),
                   pl.BlockSpec((tm, LANE), lambda i: (i, 0))],
        out_shape=[jax.ShapeDtypeStruct((m, LANE), F32),
                   jax.ShapeDtypeStruct((m, LANE), jnp.int32)],
        compiler_params=_params("parallel"),
        name="router",
    )(x16, w_pad)


def _combine_ln_kernel(r_ref, y0_ref, y1_ref, gate_ref, g_ref, b_ref, o_ref, o16_ref):
    gates = gate_ref[...]
    ff = gates[:, 0:1] * y0_ref[...].astype(F32) + gates[:, 1:2] * y1_ref[...].astype(F32)
    y = _layer_norm(DEEPNORM_ALPHA * r_ref[...] + ff, g_ref[...], b_ref[...])
    o_ref[...] = y
    o16_ref[...] = y.astype(BF16)


def _combine_ln(resid, y0, y1, gates, g, b, *, tm):
    m, d = resid.shape
    tm = min(tm, m)
    row = pl.BlockSpec((tm, d), lambda i: (i, 0))
    vec = pl.BlockSpec((1, d), lambda i: (0, 0))
    return pl.pallas_call(
        _combine_ln_kernel,
        grid=(m // tm,),
        in_specs=[row, row, row, pl.BlockSpec((tm, LANE), lambda i: (i, 0)), vec, vec],
        out_specs=[row, row],
        out_shape=[jax.ShapeDtypeStruct((m, d), F32), jax.ShapeDtypeStruct((m, d), BF16)],
        compiler_params=_params("parallel"),
        name="combine_ln",
    )(resid, y0, y1, gates, g.reshape(1, d), b.reshape(1, d))


FFN_TILE_F = 512
FFN_SUB_ROWS = 256
MOE_TILE_ROWS = 1024
GLA_PROJ_TILES = 5
GLA_BLOCK = 1024


def _cast_cols_kernel(w_ref, o_ref):
    n = w_ref.shape[1]
    o_ref[:, :n] = w_ref[...].astype(BF16)
    if o_ref.shape[1] > n:
        o_ref[:, n:] = jnp.zeros((o_ref.shape[0], o_ref.shape[1] - n), BF16)


def _bf16_cols_padded(w, n_out, *, tr=256):
    n = w.shape[-1]
    lead = w.shape[:-1]
    w2d = w.reshape(-1, n)
    rows = w2d.shape[0]
    out = pl.pallas_call(
        _cast_cols_kernel,
        grid=(rows // tr,),
        in_specs=[pl.BlockSpec((tr, n), lambda i: (i, 0))],
        out_specs=pl.BlockSpec((tr, n_out), lambda i: (i, 0)),
        out_shape=jax.ShapeDtypeStruct((rows, n_out), BF16),
        compiler_params=_params("parallel"),
        name="cast_cols",
    )(w2d)
    return out.reshape(*lead, n_out)


def _cast_rows_kernel(w_ref, o_ref, *, tail):
    r = pl.program_id(1)
    last = pl.num_programs(1) - 1

    @pl.when(r < last)
    def _():
        o_ref[...] = w_ref[...].astype(BF16)

    @pl.when(r == last)
    def _():
        o_ref[:tail, :] = w_ref[:tail, :].astype(BF16)
        if tail < o_ref.shape[0]:
            o_ref[tail:, :] = jnp.zeros((o_ref.shape[0] - tail, o_ref.shape[1]), BF16)


def _bf16_rows_padded(w, f_out, *, tr=512):
    f, d = w.shape[-2:]
    w3d = w.reshape(-1, f, d)
    groups = w3d.shape[0]
    n_blocks = f_out // tr
    tail = f - (n_blocks - 1) * tr
    assert f_out % tr == 0 and 0 < tail <= tr
    return pl.pallas_call(
        functools.partial(_cast_rows_kernel, tail=tail),
        grid=(groups, n_blocks),
        in_specs=[pl.BlockSpec((None, tr, d), lambda g, r: (g, r, 0))],
        out_specs=pl.BlockSpec((None, tr, d), lambda g, r: (g, r, 0)),
        out_shape=jax.ShapeDtypeStruct((groups, f_out, d), BF16),
        compiler_params=_params("parallel", "arbitrary"),
        name="cast_rows",
    )(w3d)


def _bf16_padded(w, axis, size):
    pad = [(0, 0)] * w.ndim
    pad[axis % w.ndim] = (0, size - w.shape[axis])
    return jnp.pad(w.astype(BF16), pad)


def _gla_proj_cols(w_in):
    main = w_in.shape[-1] - 2 * B_GATE_RANK
    return _round_up(main + LANE, GLA_PROJ_TILES * MXU_DIM)


def _gmlp_layer(x, x16, layer, w_in, v_ln_g, v_ln_b, w_s, b_s, w_out, ln_g, ln_b):
    z = _mm_act(x16, w_in, layer, gelu=True, out_dtype=BF16, tm=1024, tn=1024)
    h = _spatial_gate(z, v_ln_g, v_ln_b, w_s.astype(BF16), b_s.T, chunks=2)
    return _mm_ln(h, w_out, layer, x, ln_g, ln_b, tm=512, sub=128)


def _gla_layer(x, x16, bsz, layer, w_in, w_gk_f, bias_f, w_gk_b, bias_b, norm_g, w_out, ln_g, ln_b):
    t, d = x.shape
    seq = t // bsz
    key_dim = w_gk_f.shape[1]
    cols = w_in.shape[2]
    proj = _mm_act(x16, w_in, layer, gelu=False, out_dtype=F32, tm=1024, tn=cols // GLA_PROJ_TILES)
    proj = proj.reshape(bsz, seq, cols)
    wf = jnp.zeros((LANE, key_dim), BF16).at[:B_GATE_RANK].set(w_gk_f.astype(BF16))
    wb = jnp.zeros((LANE, key_dim), BF16).at[B_GATE_RANK:2 * B_GATE_RANK].set(w_gk_b.astype(BF16))
    o_fwd = _gla_direction(proj, wf, bias_f, reverse=False, blk=GLA_BLOCK)
    h = _gla_direction(proj, wb, bias_b, reverse=True, blk=GLA_BLOCK, o_fwd=o_fwd, norm_g=norm_g)
    return _mm_ln(h.reshape(t, -1), w_out, layer, x, ln_g, ln_b, tm=512, sub=128)


def _dense_ffn_layer(x, x16, layer, w1, w3, w2, ln_g, ln_b, *, tm):
    t = x.shape[0]
    tm = min(tm, t)
    n_tiles = t // tm
    return _ffn(x16, w1, w3, w2, jnp.full((n_tiles,), layer, jnp.int32), jnp.ones((n_tiles,), jnp.int32),
                tm=tm, tf=FFN_TILE_F, ln=(x, ln_g, ln_b))


def _moe_layer(x, x16, layer, router, w1, w3, w2, ln_g, ln_b, *, tm):
    t, d = x.shape
    tm = min(tm, t)
    router_p = jnp.pad(router.astype(BF16), ((0, 0), (0, LANE - N_EXPERTS)))
    gates, idx = _router(x16, router_p, tm=1024)

    pair_expert = idx[:, :TOP_K].reshape(-1)
    onehot = (pair_expert[:, None] == jnp.arange(N_EXPERTS, dtype=jnp.int32)[None, :]).astype(jnp.int32)
    rank = jnp.take_along_axis(jnp.cumsum(onehot, axis=0) - onehot, pair_expert[:, None], axis=1)[:, 0]
    counts = jnp.sum(onehot, axis=0)
    padded = (counts + tm - 1) // tm * tm
    ends = jnp.cumsum(padded)
    starts = ends - padded
    pair_row = starts[pair_expert] + rank
    n_tiles = (t * TOP_K) // tm + N_EXPERTS
    rows = n_tiles * tm
    row_token = (jnp.arange(rows, dtype=jnp.int32) % t).at[pair_row].set(
        jnp.arange(t * TOP_K, dtype=jnp.int32) // TOP_K)
    tile_start = jnp.arange(n_tiles, dtype=jnp.int32) * tm
    tile_valid = (tile_start < ends[-1]).astype(jnp.int32)
    tile_expert = jnp.minimum(jnp.sum((tile_start[:, None] >= ends[None, :]).astype(jnp.int32), axis=1),
                              N_EXPERTS - 1)
    last_expert = jnp.max(jnp.where(counts > 0, jnp.arange(N_EXPERTS, dtype=jnp.int32), 0))
    tile_expert = jnp.where(tile_valid != 0, tile_expert, last_expert)

    xs = x16.at[row_token].get(mode="promise_in_bounds")
    ys = _ffn(xs, w1, w3, w2, tile_expert + layer * N_EXPERTS, tile_valid, tm=tm, tf=FFN_TILE_F)
    pos = pair_row.reshape(t, TOP_K)
    y0 = ys.at[pos[:, 0]].get(mode="promise_in_bounds")
    y1 = ys.at[pos[:, 1]].get(mode="promise_in_bounds")
    return _combine_ln(x, y0, y1, gates, ln_g, ln_b, tm=512)


def kernel(x, a_w_in, a_v_ln_g, a_v_ln_b, a_w_s, a_b_s, a_w_out, b_w_in, b_w_gk_f, b_gk_bias_f, b_w_gk_b, b_gk_bias_b, b_norm_g, b_w_out, ffn_w1, ffn_w3, ffn_w2, moe_router, moe_w1, moe_w3, moe_w2, ln_g, ln_b):
    bsz, seq, d = x.shape
    x = x.reshape(bsz * seq, d)
    x16 = x.astype(BF16)

    f_pad = _round_up(ffn_w1.shape[-1], FFN_TILE_F)
    a_w_in = _bf16_cols_padded(a_w_in, a_w_in.shape[-1])
    a_w_out = _bf16_cols_padded(a_w_out, a_w_out.shape[-1])
    b_w_in = _bf16_padded(b_w_in, -1, _gla_proj_cols(b_w_in))
    b_w_out = _bf16_cols_padded(b_w_out, b_w_out.shape[-1])
    ffn_w1 = _bf16_cols_padded(ffn_w1, f_pad)
    ffn_w3 = _bf16_cols_padded(ffn_w3, f_pad)
    ffn_w2 = _bf16_rows_padded(ffn_w2, f_pad)
    moe_w1 = _bf16_cols_padded(moe_w1, f_pad).reshape(-1, d, f_pad)
    moe_w3 = _bf16_cols_padded(moe_w3, f_pad).reshape(-1, d, f_pad)
    moe_w2 = _bf16_rows_padded(moe_w2, f_pad)
    x16, moe_w1, moe_w3, moe_w2 = lax.optimization_barrier((x16, moe_w1, moe_w3, moe_w2))

    for i in range(DEPTH):
        j = i // 2
        if i % 2 == 0:
            x, x16 = _gmlp_layer(x, x16, j, a_w_in, a_v_ln_g[j], a_v_ln_b[j], a_w_s[j], a_b_s[j],
                                 a_w_out, ln_g[i, 0], ln_b[i, 0])
            x, x16 = _dense_ffn_layer(x, x16, j, ffn_w1, ffn_w3, ffn_w2, ln_g[i, 1], ln_b[i, 1], tm=512)
        else:
            x, x16 = _gla_layer(x, x16, bsz, j, b_w_in, b_w_gk_f[j], b_gk_bias_f[j], b_w_gk_b[j],
                                b_gk_bias_b[j], b_norm_g[j], b_w_out, ln_g[i, 0], ln_b[i, 0])
            x, x16 = _moe_layer(x, x16, j, moe_router[j], moe_w1, moe_w3, moe_w2,
                                ln_g[i, 1], ln_b[i, 1], tm=MOE_TILE_ROWS)
    return x.reshape(bsz, seq, d)
```

```python
import functools

import jax
import jax.numpy as jnp
from jax import lax
from jax.experimental import pallas as pl
from jax.experimental.pallas import tpu as pltpu

F32 = jnp.float32
BF16 = jnp.bfloat16

DEPTH = 4
A_CHUNK = 128
A_GROUPS = 8
B_HEADS = 4
B_GATE_RANK = 16
B_GATE_NORMALIZER = 16.0
B_CHUNK = 64
N_EXPERTS = 8
TOP_K = 2
DEEPNORM_ALPHA = (2.0 * DEPTH) ** 0.25
LN_EPS = 1e-5
RMS_EPS = 1e-5

LANE = 128
MXU_DIM = 256
VMEM_LIMIT_BYTES = 56 * 1024 * 1024


def _round_up(n, m):
    return (n + m - 1) // m * m


def _params(*semantics):
    return pltpu.CompilerParams(dimension_semantics=semantics,
                                vmem_limit_bytes=VMEM_LIMIT_BYTES)


def _layer_norm(y, g, b):
    mu = jnp.mean(y, axis=-1, keepdims=True)
    d = y - mu
    var = jnp.mean(d * d, axis=-1, keepdims=True)
    return d * lax.rsqrt(var + LN_EPS) * g + b


def _mm_act_kernel(x_ref, w_ref, o_ref, *, gelu):
    acc = jnp.dot(x_ref[...], w_ref[...], preferred_element_type=F32)
    if gelu:
        acc = 0.5 * acc * (1.0 + lax.erf(acc * (2.0 ** -0.5)))
    o_ref[...] = acc.astype(o_ref.dtype)


def _mm_act(x, w, layer, *, gelu, out_dtype, tm, tn):
    m, k = x.shape
    n = w.shape[2]
    tm = min(tm, m)
    return pl.pallas_call(
        functools.partial(_mm_act_kernel, gelu=gelu),
        grid=(m // tm, n // tn),
        in_specs=[pl.BlockSpec((tm, k), lambda i, j: (i, 0)),
                  pl.BlockSpec((None, k, tn), lambda i, j: (layer, 0, j))],
        out_specs=pl.BlockSpec((tm, tn), lambda i, j: (i, j)),
        out_shape=jax.ShapeDtypeStruct((m, n), out_dtype),
        compiler_params=_params("parallel", "arbitrary"),
        name="mm_act",
    )(x, w)


def _mm_ln_kernel(h_ref, w_ref, r_ref, g_ref, b_ref, o_ref, o16_ref, *, sub):
    for s in range(h_ref.shape[0] // sub):
        rows = slice(s * sub, (s + 1) * sub)
        acc = jnp.dot(h_ref[rows, :], w_ref[...], preferred_element_type=F32)
        y = _layer_norm(DEEPNORM_ALPHA * r_ref[rows, :] + acc, g_ref[...], b_ref[...])
        o_ref[rows, :] = y
        o16_ref[rows, :] = y.astype(BF16)


def _mm_ln(h, w, layer, resid, g, b, *, tm, sub):
    m, kdim = h.shape
    n = w.shape[2]
    tm = min(tm, m)
    return pl.pallas_call(
        functools.partial(_mm_ln_kernel, sub=sub),
        grid=(m // tm,),
        in_specs=[pl.BlockSpec((tm, kdim), lambda i: (i, 0)),
                  pl.BlockSpec((None, kdim, n), lambda i: (layer, 0, 0),
                               pipeline_mode=pl.Buffered(1)),
                  pl.BlockSpec((tm, n), lambda i: (i, 0)),
                  pl.BlockSpec((1, n), lambda i: (0, 0)),
                  pl.BlockSpec((1, n), lambda i: (0, 0))],
        out_specs=[pl.BlockSpec((tm, n), lambda i: (i, 0)),
                   pl.BlockSpec((tm, n), lambda i: (i, 0))],
        out_shape=[jax.ShapeDtypeStruct((m, n), F32),
                   jax.ShapeDtypeStruct((m, n), BF16)],
        compiler_params=_params("parallel"),
        name="mm_ln",
    )(h, w, resid, g.reshape(1, n), b.reshape(1, n))


def _spatial_gate_kernel(z_ref, g_ref, b_ref, ws_ref, bs_ref, o_ref, *, width, chunks):
    gdim = width // A_GROUPS
    v = z_ref[:, width:].astype(F32)
    vn = _layer_norm(v, g_ref[...], b_ref[...]).astype(BF16)
    for grp in range(A_GROUPS):
        w = ws_ref[grp]
        bias = bs_ref[:, grp:grp + 1]
        cols = slice(grp * gdim, (grp + 1) * gdim)
        for c in range(chunks):
            rows = slice(c * A_CHUNK, (c + 1) * A_CHUNK)
            s = jnp.dot(w, vn[rows, cols], preferred_element_type=F32) + bias
            u = z_ref[rows, cols].astype(F32)
            o_ref[rows, cols] = (u * s).astype(BF16)


def _spatial_gate(z, ln_g, ln_b, w_s, b_s_t, *, chunks):
    m = z.shape[0]
    width = z.shape[1] // 2
    tm = chunks * A_CHUNK
    return pl.pallas_call(
        functools.partial(_spatial_gate_kernel, width=width, chunks=chunks),
        grid=(m // tm,),
        in_specs=[pl.BlockSpec((tm, 2 * width), lambda i: (i, 0)),
                  pl.BlockSpec((1, width), lambda i: (0, 0)),
                  pl.BlockSpec((1, width), lambda i: (0, 0)),
                  pl.BlockSpec((A_GROUPS, A_CHUNK, A_CHUNK), lambda i: (0, 0, 0)),
                  pl.BlockSpec((A_CHUNK, A_GROUPS), lambda i: (0, 0))],
        out_specs=pl.BlockSpec((tm, width), lambda i: (i, 0)),
        out_shape=jax.ShapeDtypeStruct((m, width), BF16),
        compiler_params=_params("parallel"),
        name="spatial_gate",
    )(z, ln_g.reshape(1, width), ln_b.reshape(1, width), w_s, b_s_t)


def _gla_kernel(*refs, reverse, n_chunks):
    if reverse:
        (q_ref, k_ref, v_ref, low_ref, wgk_ref, bias_ref,
         ofwd_ref, gate_ref, ng_ref, o_ref, state_ref) = refs
    else:
        q_ref, k_ref, v_ref, low_ref, wgk_ref, bias_ref, o_ref, state_ref = refs
    dk = q_ref.shape[-1]
    c_len = B_CHUNK

    @pl.when(pl.program_id(2) == 0)
    def _():
        state_ref[...] = jnp.zeros_like(state_ref)

    row = lax.broadcasted_iota(jnp.int32, (c_len, c_len), 0)
    col = lax.broadcasted_iota(jnp.int32, (c_len, c_len), 1)
    if reverse:
        keep = col > row
        mid, last = c_len // 2, 0
    else:
        keep = col <= row
        mid, last = c_len // 2 - 1, c_len - 1

    pre = jnp.dot(low_ref[0].astype(BF16), wgk_ref[...], preferred_element_type=F32) + bias_ref[...]
    log_a_all = jax.nn.log_sigmoid(pre) / B_GATE_NORMALIZER

    n_rows = log_a_all.shape[0]
    pos = lax.broadcasted_iota(jnp.int32, log_a_all.shape, 0) % c_len
    b_all = log_a_all
    step = 1
    while step < c_len:
        if reverse:
            shifted = pltpu.roll(b_all, n_rows - step, axis=0)
            b_all = b_all + jnp.where(pos < c_len - step, shifted, 0.0)
        else:
            shifted = pltpu.roll(b_all, step, axis=0)
            b_all = b_all + jnp.where(pos >= step, shifted, 0.0)
        step *= 2

    state = state_ref[...]

    for ci in range(n_chunks):
        c = (n_chunks - 1 - ci) if reverse else ci
        rows = slice(c * c_len, (c + 1) * c_len)
        q = q_ref[0, rows, :] * (dk ** -0.5)
        k = k_ref[0, rows, :]
        v = v_ref[0, rows, :].astype(BF16)
        b = b_all[rows, :]
        b_mid = b[mid:mid + 1, :]
        b_last = b[last:last + 1, :]
        q_rel = (q * jnp.exp(b - b_mid)).astype(BF16)
        k_rel = (k * jnp.exp(b_mid - b)).astype(BF16)
        scores = lax.dot_general(q_rel, k_rel, (((1,), (1,)), ((), ())),
                                 preferred_element_type=F32)
        scores = jnp.where(keep, scores, 0.0).astype(BF16)
        o = jnp.dot(scores, v, preferred_element_type=F32)
        q_inter = (q * jnp.exp(b)).astype(BF16)
        k_upd = (k * jnp.exp(b_last - b)).astype(BF16)
        o = o + lax.dot_general(q_inter, state.astype(BF16), (((1,), (1,)), ((), ())),
                                preferred_element_type=F32)
        state = state * jnp.exp(b_last) + lax.dot_general(
            v, k_upd, (((0,), (0,)), ((), ())), preferred_element_type=F32)
        if reverse:
            o = o + ofwd_ref[0, rows, :]
            o = o * lax.rsqrt(jnp.mean(o * o, axis=-1, keepdims=True) + RMS_EPS) * ng_ref[...]
            o = o * jax.nn.silu(gate_ref[0, rows, :])
        o_ref[0, rows, :] = o.astype(o_ref.dtype)

    state_ref[...] = state


def _gla_direction(proj, wgk, bias, *, reverse, blk, o_fwd=None, norm_g=None):
    bsz, seq, _ = proj.shape
    dk = wgk.shape[1] // B_HEADS
    dv = 2 * dk
    key_dim = dk * B_HEADS
    val_dim = dv * B_HEADS
    blk = min(blk, seq)
    nb = seq // blk
    low_block = (2 * key_dim + 2 * val_dim) // LANE

    def sblk(n):
        return (nb - 1 - n) if reverse else n

    in_specs = [
        pl.BlockSpec((1, blk, dk), lambda b, h, n: (b, sblk(n), h)),
        pl.BlockSpec((1, blk, dk), lambda b, h, n: (b, sblk(n), key_dim // dk + h)),
        pl.BlockSpec((1, blk, dv), lambda b, h, n: (b, sblk(n), 2 * key_dim // dv + h)),
        pl.BlockSpec((1, blk, LANE), lambda b, h, n: (b, sblk(n), low_block)),
        pl.BlockSpec((LANE, dk), lambda b, h, n: (0, h)),
        pl.BlockSpec((1, dk), lambda b, h, n: (0, h)),
    ]
    args = [proj, proj, proj, proj, wgk, bias.reshape(1, key_dim)]
    if reverse:
        in_specs += [
            pl.BlockSpec((1, blk, dv), lambda b, h, n: (b, sblk(n), h)),
            pl.BlockSpec((1, blk, dv), lambda b, h, n: (b, sblk(n), (2 * key_dim + val_dim) // dv + h)),
            pl.BlockSpec((1, dv), lambda b, h, n: (0, 0)),
        ]
        args += [o_fwd, proj, norm_g.reshape(1, dv)]
    return pl.pallas_call(
        functools.partial(_gla_kernel, reverse=reverse, n_chunks=blk // B_CHUNK),
        grid=(bsz, B_HEADS, nb),
        in_specs=in_specs,
        out_specs=pl.BlockSpec((1, blk, dv), lambda b, h, n: (b, sblk(n), h)),
        out_shape=jax.ShapeDtypeStruct((bsz, seq, val_dim), BF16 if reverse else F32),
        scratch_shapes=[pltpu.VMEM((dv, dk), F32)],
        compiler_params=_params("parallel", "parallel", "arbitrary"),
        name="gla_bwd" if reverse else "gla_fwd",
    )(*args)


def _ffn_kernel(te_ref, tv_ref, *refs, fuse_ln, sub):
    if fuse_ln:
        x_ref, w1_ref, w3_ref, w2_ref, r_ref, g_ref, b_ref, o_ref, o16_ref, acc_ref = refs
    else:
        x_ref, w1_ref, w3_ref, w2_ref, o_ref, acc_ref = refs
    i = pl.program_id(0)
    j = pl.program_id(1)
    last = pl.num_programs(1) - 1
    valid = tv_ref[i] != 0

    @pl.when(j == 0)
    def _():
        acc_ref[...] = jnp.zeros_like(acc_ref)

    def step(matmul, finish):
        n_rows = x_ref.shape[0]
        size = sub if (finish and fuse_ln) else n_rows
        for s in range(n_rows // size):
            rows = slice(s * size, (s + 1) * size)
            acc = acc_ref[rows, :]
            if matmul:
                x = x_ref[rows, :]
                a = jnp.dot(x, w1_ref[...], preferred_element_type=F32)
                c = jnp.dot(x, w3_ref[...], preferred_element_type=F32)
                h = (jax.nn.silu(a) * c).astype(BF16)
                acc = acc + jnp.dot(h, w2_ref[...], preferred_element_type=F32)
            if not finish:
                acc_ref[rows, :] = acc
            elif fuse_ln:
                y = _layer_norm(DEEPNORM_ALPHA * r_ref[rows, :] + acc, g_ref[...], b_ref[...])
                o_ref[rows, :] = y
                o16_ref[rows, :] = y.astype(BF16)
            else:
                o_ref[rows, :] = acc.astype(o_ref.dtype)

    pl.when(jnp.logical_and(valid, j < last))(functools.partial(step, True, False))
    pl.when(jnp.logical_and(valid, j == last))(functools.partial(step, True, True))
    pl.when(jnp.logical_and(jnp.logical_not(valid), j == last))(functools.partial(step, False, True))


def _ffn(xs, w1, w3, w2, tile_expert, tile_valid, *, tm, tf, ln=None):
    rows, d = xs.shape
    f = w1.shape[2]
    nj = f // tf
    fuse_ln = ln is not None

    def wcol(i, j, te, tv):
        return jnp.where(tv[i] != 0, j, nj - 1)

    in_specs = [
        pl.BlockSpec((tm, d), lambda i, j, te, tv: (i, 0)),
        pl.BlockSpec((None, d, tf), lambda i, j, te, tv: (te[i], 0, wcol(i, j, te, tv))),
        pl.BlockSpec((None, d, tf), lambda i, j, te, tv: (te[i], 0, wcol(i, j, te, tv))),
        pl.BlockSpec((None, tf, d), lambda i, j, te, tv: (te[i], wcol(i, j, te, tv), 0)),
    ]
    args = [xs, w1, w3, w2]
    out_specs = [pl.BlockSpec((tm, d), lambda i, j, te, tv: (i, 0))]
    out_shape = [jax.ShapeDtypeStruct((rows, d), F32 if fuse_ln else BF16)]
    if fuse_ln:
        resid, g, b = ln
        in_specs += [pl.BlockSpec((tm, d), lambda i, j, te, tv: (i, 0)),
                     pl.BlockSpec((1, d), lambda i, j, te, tv: (0, 0)),
                     pl.BlockSpec((1, d), lambda i, j, te, tv: (0, 0))]
        args += [resid, g.reshape(1, d), b.reshape(1, d)]
        out_specs.append(pl.BlockSpec((tm, d), lambda i, j, te, tv: (i, 0)))
        out_shape.append(jax.ShapeDtypeStruct((rows, d), BF16))
    out = pl.pallas_call(
        functools.partial(_ffn_kernel, fuse_ln=fuse_ln, sub=min(FFN_SUB_ROWS, tm)),
        grid_spec=pltpu.PrefetchScalarGridSpec(
            num_scalar_prefetch=2,
            grid=(rows // tm, nj),
            in_specs=in_specs,
            out_specs=out_specs,
            scratch_shapes=[pltpu.VMEM((tm, d), F32)]),
        out_shape=out_shape,
        compiler_params=_params("parallel", "arbitrary"),
        name="ffn_ln" if fuse_ln else "ffn_experts",
    )(tile_expert, tile_valid, *args)
    return out if fuse_ln else out[0]


def _router_kernel(x_ref, w_ref, gate_ref, idx_ref):
    logits = jnp.dot(x_ref[...], w_ref[...], preferred_element_type=F32)
    col = lax.broadcasted_iota(jnp.int32, logits.shape, 1)
    neg = jnp.float32(-jnp.inf)
    logits = jnp.where(col < N_EXPERTS, logits, neg)
    m1 = jnp.max(logits, axis=-1, keepdims=True)
    i1 = jnp.min(jnp.where(logits == m1, col, LANE), axis=-1, keepdims=True)
    rest = jnp.where(col == i1, neg, logits)
    m2 = jnp.max(rest, axis=-1, keepdims=True)
    i2 = jnp.min(jnp.where(rest == m2, col, LANE), axis=-1, keepdims=True)
    e1 = jnp.exp(m1 - m1)
    e2 = jnp.exp(m2 - m1)
    total = e1 + e2
    gate_ref[...] = jnp.where(col == 0, e1 / total, jnp.where(col == 1, e2 / total, 0.0))
    idx_ref[...] = jnp.where(col == 0, i1, jnp.where(col == 1, i2, 0))


def _router(x16, w_pad, *, tm):
    m, d = x16.shape
    tm = min(tm, m)
    return pl.pallas_call(
        _router_kernel,
        grid=(m // tm,),
        in_specs=[pl.BlockSpec((tm, d), lambda i: (i, 0)),
                  pl.BlockSpec((d, LANE), lambda i: (0, 0))],
        out_specs=[pl.BlockSpec((tm, LANE), lambda i: (i, 0)),
                   pl.BlockSpec((tm, LANE), lambda i: (i, 0))],
        out_shape=[jax.ShapeDtypeStruct((m, LANE), F32),
                   jax.ShapeDtypeStruct((m, LANE), jnp.int32)],
        compiler_params=_params("parallel"),
        name="router",
    )(x16, w_pad)


def _combine_ln_kernel(r_ref, y0_ref, y1_ref, gate_ref, g_ref, b_ref, o_ref, o16_ref):
    gates = gate_ref[...]
    ff = gates[:, 0:1] * y0_ref[...].astype(F32) + gates[:, 1:2] * y1_ref[...].astype(F32)
    y = _layer_norm(DEEPNORM_ALPHA * r_ref[...] + ff, g_ref[...], b_ref[...])
    o_ref[...] = y
    o16_ref[...] = y.astype(BF16)


def _combine_ln(resid, y0, y1, gates, g, b, *, tm):
    m, d = resid.shape
    tm = min(tm, m)
    row = pl.BlockSpec((tm, d), lambda i: (i, 0))
    vec = pl.BlockSpec((1, d), lambda i: (0, 0))
    return pl.pallas_call(
        _combine_ln_kernel,
        grid=(m // tm,),
        in_specs=[row, row, row, pl.BlockSpec((tm, LANE), lambda i: (i, 0)), vec, vec],
        out_specs=[row, row],
        out_shape=[jax.ShapeDtypeStruct((m, d), F32), jax.ShapeDtypeStruct((m, d), BF16)],
        compiler_params=_params("parallel"),
        name="combine_ln",
    )(resid, y0, y1, gates, g.reshape(1, d), b.reshape(1, d))


FFN_TILE_F = 512
FFN_SUB_ROWS = 256
MOE_TILE_ROWS = 1024
GLA_PROJ_TILES = 5
GLA_BLOCK = 1024


def _cast_cols_kernel(w_ref, o_ref):
    n = w_ref.shape[1]
    o_ref[:, :n] = w_ref[...].astype(BF16)
    if o_ref.shape[1] > n:
        o_ref[:, n:] = jnp.zeros((o_ref.shape[0], o_ref.shape[1] - n), BF16)


def _bf16_cols_padded(w, n_out, *, tr=256):
    n = w.shape[-1]
    lead = w.shape[:-1]
    w2d = w.reshape(-1, n)
    rows = w2d.shape[0]
    out = pl.pallas_call(
        _cast_cols_kernel,
        grid=(rows // tr,),
        in_specs=[pl.BlockSpec((tr, n), lambda i: (i, 0))],
        out_specs=pl.BlockSpec((tr, n_out), lambda i: (i, 0)),
        out_shape=jax.ShapeDtypeStruct((rows, n_out), BF16),
        compiler_params=_params("parallel"),
        name="cast_cols",
    )(w2d)
    return out.reshape(*lead, n_out)


def _cast_rows_kernel(w_ref, o_ref, *, tail):
    r = pl.program_id(1)
    last = pl.num_programs(1) - 1

    @pl.when(r < last)
    def _():
        o_ref[...] = w_ref[...].astype(BF16)

    @pl.when(r == last)
    def _():
        o_ref[:tail, :] = w_ref[:tail, :].astype(BF16)
        if tail < o_ref.shape[0]:
            o_ref[tail:, :] = jnp.zeros((o_ref.shape[0] - tail, o_ref.shape[1]), BF16)


def _bf16_rows_padded(w, f_out, *, tr=512):
    f, d = w.shape[-2:]
    w3d = w.reshape(-1, f, d)
    groups = w3d.shape[0]
    n_blocks = f_out // tr
    tail = f - (n_blocks - 1) * tr
    assert f_out % tr == 0 and 0 < tail <= tr
    return pl.pallas_call(
        functools.partial(_cast_rows_kernel, tail=tail),
        grid=(groups, n_blocks),
        in_specs=[pl.BlockSpec((None, tr, d), lambda g, r: (g, r, 0))],
        out_specs=pl.BlockSpec((None, tr, d), lambda g, r: (g, r, 0)),
        out_shape=jax.ShapeDtypeStruct((groups, f_out, d), BF16),
        compiler_params=_params("parallel", "arbitrary"),
        name="cast_rows",
    )(w3d)


def _bf16_padded(w, axis, size):
    pad = [(0, 0)] * w.ndim
    pad[axis % w.ndim] = (0, size - w.shape[axis])
    return jnp.pad(w.astype(BF16), pad)


def _gla_proj_cols(w_in):
    main = w_in.shape[-1] - 2 * B_GATE_RANK
    return _round_up(main + LANE, GLA_PROJ_TILES * MXU_DIM)


def _gmlp_layer(x, x16, layer, w_in, v_ln_g, v_ln_b, w_s, b_s, w_out, ln_g, ln_b):
    z = _mm_act(x16, w_in, layer, gelu=True, out_dtype=BF16, tm=1024, tn=1024)
    h = _spatial_gate(z, v_ln_g, v_ln_b, w_s.astype(BF16), b_s.T, chunks=2)
    return _mm_ln(h, w_out, layer, x, ln_g, ln_b, tm=512, sub=128)


def _gla_layer(x, x16, bsz, layer, w_in, w_gk_f, bias_f, w_gk_b, bias_b, norm_g, w_out, ln_g, ln_b):
    t, d = x.shape
    seq = t // bsz
    key_dim = w_gk_f.shape[1]
    cols = w_in.shape[2]
    proj = _mm_act(x16, w_in, layer, gelu=False, out_dtype=F32, tm=1024, tn=cols // GLA_PROJ_TILES)
    proj = proj.reshape(bsz, seq, cols)
    wf = jnp.zeros((LANE, key_dim), BF16).at[:B_GATE_RANK].set(w_gk_f.astype(BF16))
    wb = jnp.zeros((LANE, key_dim), BF16).at[B_GATE_RANK:2 * B_GATE_RANK].set(w_gk_b.astype(BF16))
    o_fwd = _gla_direction(proj, wf, bias_f, reverse=False, blk=GLA_BLOCK)
    h = _gla_direction(proj, wb, bias_b, reverse=True, blk=GLA_BLOCK, o_fwd=o_fwd, norm_g=norm_g)
    return _mm_ln(h.reshape(t, -1), w_out, layer, x, ln_g, ln_b, tm=512, sub=128)


def _dense_ffn_layer(x, x16, layer, w1, w3, w2, ln_g, ln_b, *, tm):
    t = x.shape[0]
    tm = min(tm, t)
    n_tiles = t // tm
    return _ffn(x16, w1, w3, w2, jnp.full((n_tiles,), layer, jnp.int32), jnp.ones((n_tiles,), jnp.int32),
                tm=tm, tf=FFN_TILE_F, ln=(x, ln_g, ln_b))


def _moe_layer(x, x16, layer, router, w1, w3, w2, ln_g, ln_b, *, tm):
    t, d = x.shape
    tm = min(tm, t)
    router_p = jnp.pad(router.astype(BF16), ((0, 0), (0, LANE - N_EXPERTS)))
    gates, idx = _router(x16, router_p, tm=1024)

    pair_expert = idx[:, :TOP_K].reshape(-1)
    onehot = (pair_expert[:, None] == jnp.arange(N_EXPERTS, dtype=jnp.int32)[None, :]).astype(jnp.int32)
    rank = jnp.take_along_axis(jnp.cumsum(onehot, axis=0) - onehot, pair_expert[:, None], axis=1)[:, 0]
    counts = jnp.sum(onehot, axis=0)
    padded = (counts + tm - 1) // tm * tm
    ends = jnp.cumsum(padded)
    starts = ends - padded
    pair_row = starts[pair_expert] + rank
    n_tiles = (t * TOP_K) // tm + N_EXPERTS
    rows = n_tiles * tm
    row_token = (jnp.arange(rows, dtype=jnp.int32) % t).at[pair_row].set(
        jnp.arange(t * TOP_K, dtype=jnp.int32) // TOP_K,
        unique_indices=True, mode="promise_in_bounds")
    tile_start = jnp.arange(n_tiles, dtype=jnp.int32) * tm
    tile_valid = (tile_start < ends[-1]).astype(jnp.int32)
    tile_expert = jnp.minimum(jnp.sum((tile_start[:, None] >= ends[None, :]).astype(jnp.int32), axis=1),
                              N_EXPERTS - 1)
    last_expert = jnp.max(jnp.where(counts > 0, jnp.arange(N_EXPERTS, dtype=jnp.int32), 0))
    tile_expert = jnp.where(tile_valid != 0, tile_expert, last_expert)

    xs = x16.at[row_token].get(mode="promise_in_bounds")
    ys = _ffn(xs, w1, w3, w2, tile_expert + layer * N_EXPERTS, tile_valid, tm=tm, tf=FFN_TILE_F)
    pos = pair_row.reshape(t, TOP_K)
    y0 = ys.at[pos[:, 0]].get(mode="promise_in_bounds")
    y1 = ys.at[pos[:, 1]].get(mode="promise_in_bounds")
    return _combine_ln(x, y0, y1, gates, ln_g, ln_b, tm=512)


def kernel(x, a_w_in, a_v_ln_g, a_v_ln_b, a_w_s, a_b_s, a_w_out, b_w_in, b_w_gk_f, b_gk_bias_f, b_w_gk_b, b_gk_bias_b, b_norm_g, b_w_out, ffn_w1, ffn_w3, ffn_w2, moe_router, moe_w1, moe_w3, moe_w2, ln_g, ln_b):
    bsz, seq, d = x.shape
    x = x.reshape(bsz * seq, d)
    x16 = x.astype(BF16)

    f_pad = _round_up(ffn_w1.shape[-1], FFN_TILE_F)
    a_w_in = _bf16_cols_padded(a_w_in, a_w_in.shape[-1])
    a_w_out = _bf16_cols_padded(a_w_out, a_w_out.shape[-1])
    b_w_in = _bf16_padded(b_w_in, -1, _gla_proj_cols(b_w_in))
    b_w_out = _bf16_cols_padded(b_w_out, b_w_out.shape[-1])
    ffn_w1 = _bf16_cols_padded(ffn_w1, f_pad)
    ffn_w3 = _bf16_cols_padded(ffn_w3, f_pad)
    ffn_w2 = _bf16_rows_padded(ffn_w2, f_pad)
    moe_w1 = _bf16_cols_padded(moe_w1, f_pad).reshape(-1, d, f_pad)
    moe_w3 = _bf16_cols_padded(moe_w3, f_pad).reshape(-1, d, f_pad)
    moe_w2 = _bf16_rows_padded(moe_w2, f_pad)
    x16, moe_w1, moe_w3, moe_w2 = lax.optimization_barrier((x16, moe_w1, moe_w3, moe_w2))

    for i in range(DEPTH):
        j = i // 2
        if i % 2 == 0:
            x, x16 = _gmlp_layer(x, x16, j, a_w_in, a_v_ln_g[j], a_v_ln_b[j], a_w_s[j], a_b_s[j],
                                 a_w_out, ln_g[i, 0], ln_b[i, 0])
            x, x16 = _dense_ffn_layer(x, x16, j, ffn_w1, ffn_w3, ffn_w2, ln_g[i, 1], ln_b[i, 1], tm=512)
        else:
            x, x16 = _gla_layer(x, x16, bsz, j, b_w_in, b_w_gk_f[j], b_gk_bias_f[j], b_w_gk_b[j],
                                b_gk_bias_b[j], b_norm_g[j], b_w_out, ln_g[i, 0], ln_b[i, 0])
            x, x16 = _moe_layer(x, x16, j, moe_router[j], moe_w1, moe_w3, moe_w2,
                                ln_g[i, 1], ln_b[i, 1], tm=MOE_TILE_ROWS)
    return x.reshape(bsz, seq, d)
```
